```python
import math
import jax, jax.numpy as jnp
from jax import lax
import numpy as np

D_MODEL = 1024
BATCH = 8
SEQ = 8192
DEPTH = 1

HEAD_DIM = 64
NSA_HEADS = D_MODEL // 2 // HEAD_DIM
NSA_KV_GROUPS = 2
CMP_BLOCK = 32
CMP_STRIDE = 16
CMP_HIDDEN = 256
SEL_BLOCK = 64
SEL_TOPN = 16
NSA_WINDOW = 512
SWA_HEADS = D_MODEL // 2 // HEAD_DIM
SWA_KV_HEADS = 2
SWA_WINDOW = 128
REL_BUCKETS = 32
REL_MAX_DIST = 128
MEM_LEN = 256
XATTN_HEADS = 4
XATTN_HEAD_DIM = D_MODEL // XATTN_HEADS
D_FF = ((8 * D_MODEL // 3 + 127) // 128) * 128
NSA_WIDTH = NSA_HEADS * HEAD_DIM
NSA_KV_WIDTH = NSA_KV_GROUPS * HEAD_DIM
SWA_WIDTH = SWA_HEADS * HEAD_DIM
SWA_KV_WIDTH = SWA_KV_HEADS * HEAD_DIM
IN_WIDTH = NSA_WIDTH + 6 * NSA_KV_WIDTH + 3 * NSA_HEADS + SWA_WIDTH + 2 * SWA_KV_WIDTH + 2 * D_MODEL
Q_BLOCK = 128
EPS = 1e-6
NEG_INF = -1e30
FORCE = 1e4

kernel_name = 'hybrid_nsa_swa_sink_macaron_layer'


def rms_norm(x, g):
    xf = x.astype(jnp.float32)
    y = xf * lax.rsqrt(jnp.mean(xf * xf, axis=-1, keepdims=True) + EPS)
    return (y * g.astype(jnp.float32)).astype(x.dtype)


def swiglu(h, w_gate, w_up, w_down):
    return (jax.nn.silu(h @ w_gate) * (h @ w_up)) @ w_down


def rel_bucket(dist):
    dist = jnp.maximum(dist, 0)
    max_exact = REL_BUCKETS // 2
    d = jnp.maximum(dist, 1).astype(jnp.float32)
    large = max_exact + (jnp.log(d / max_exact) / math.log(REL_MAX_DIST / max_exact)
                         * (REL_BUCKETS - max_exact)).astype(jnp.int32)
    large = jnp.minimum(large, REL_BUCKETS - 1)
    return jnp.where(dist < max_exact, dist, large)


def masked_softmax(logits, mask):
    logits = jnp.where(mask, logits.astype(jnp.float32), NEG_INF)
    p = jax.nn.softmax(logits, axis=-1)
    return jnp.where(mask, p, 0.0)


def sink_softmax(logits, mask, sinks):
    logits = jnp.where(mask, logits.astype(jnp.float32), NEG_INF)
    sink = jnp.broadcast_to(sinks.astype(jnp.float32), logits.shape[:-1] + (1,))
    p = jax.nn.softmax(jnp.concatenate([logits, sink], axis=-1), axis=-1)[..., :-1]
    return jnp.where(mask, p, 0.0)


def token_mixing(h, w_in, cmp_pe_k, cmp_w1_k, cmp_w2_k, cmp_pe_v, cmp_w1_v, cmp_w2_v,
                 attn_sinks, rel_bias, w_up_a, w_up_b, w_out):
    B, S, _ = h.shape
    G, HPG = NSA_KV_GROUPS, NSA_HEADS // NSA_KV_GROUPS
    KB, HPB = SWA_KV_HEADS, SWA_HEADS // SWA_KV_HEADS
    sizes = [NSA_WIDTH] + [NSA_KV_WIDTH] * 6 + [3 * NSA_HEADS, SWA_WIDTH, SWA_KV_WIDTH, SWA_KV_WIDTH,
                                               D_MODEL, D_MODEL]
    splits = [int(s) for s in np.cumsum(sizes)[:-1]]
    proj = h @ w_in
    (q_a, k_c, v_c, k_s, v_s, k_w, v_w, g_nsa, q_b, k_b, v_b, gate_a, gate_b) = jnp.split(proj, splits, axis=-1)
    scale = HEAD_DIM ** -0.5
    q_a = q_a.reshape(B, S, G, HPG, HEAD_DIM) * scale
    k_c, v_c, k_s, v_s, k_w, v_w = [a.reshape(B, S, G, HEAD_DIM) for a in (k_c, v_c, k_s, v_s, k_w, v_w)]
    g_nsa = jax.nn.sigmoid(g_nsa).reshape(B, S, 3, G, HPG)
    q_b = q_b.reshape(B, S, KB, HPB, HEAD_DIM) * scale
    k_b = k_b.reshape(B, S, KB, HEAD_DIM)
    v_b = v_b.reshape(B, S, KB, HEAD_DIM)

    n_cmp = (S - CMP_BLOCK) // CMP_STRIDE + 1
    cmp_start = jnp.arange(n_cmp) * CMP_STRIDE
    cmp_end = cmp_start + CMP_BLOCK - 1
    tok = cmp_start[:, None] + jnp.arange(CMP_BLOCK)[None, :]

    def compress(a, pe, w1, w2):
        blocks = a[:, tok] + pe[None, None, :, None, :]
        flat = blocks.transpose(0, 1, 3, 2, 4).reshape(B, n_cmp, G, CMP_BLOCK * HEAD_DIM)
        return jax.nn.gelu(flat @ w1) @ w2

    kc = compress(k_c, cmp_pe_k, cmp_w1_k, cmp_w2_k)
    vc = compress(v_c, cmp_pe_v, cmp_w1_v, cmp_w2_v)

    n_sel = S // SEL_BLOCK
    top_n = min(SEL_TOPN, n_sel)
    sel_start = jnp.arange(n_sel) * SEL_BLOCK
    ov = (jnp.minimum(cmp_start[:, None] + CMP_BLOCK, sel_start[None, :] + SEL_BLOCK)
          - jnp.maximum(cmp_start[:, None], sel_start[None, :]))
    overlap = (jnp.maximum(ov, 0) / CMP_BLOCK).astype(jnp.float32)
    ks_blk = k_s.reshape(B, n_sel, SEL_BLOCK, G, HEAD_DIM)
    vs_blk = v_s.reshape(B, n_sel, SEL_BLOCK, G, HEAD_DIM)

    k_w_pad = jnp.pad(k_w, ((0, 0), (NSA_WINDOW, 0), (0, 0), (0, 0)))
    v_w_pad = jnp.pad(v_w, ((0, 0), (NSA_WINDOW, 0), (0, 0), (0, 0)))
    k_b_pad = jnp.pad(k_b, ((0, 0), (SWA_WINDOW, 0), (0, 0), (0, 0)))
    v_b_pad = jnp.pad(v_b, ((0, 0), (SWA_WINDOW, 0), (0, 0), (0, 0)))

    bias_a = rel_bias[:, :NSA_HEADS].reshape(REL_BUCKETS, G, HPG)
    bias_b = rel_bias[:, NSA_HEADS:].reshape(REL_BUCKETS, KB, HPB)
    sinks = attn_sinks.reshape(KB, HPB, 1)
    b_ix = jnp.arange(B)[:, None, None, None]
    g_ix = jnp.arange(G)[None, None, :, None]
    blk_j = jnp.arange(n_sel)

    def query_block(c):
        s0 = c * Q_BLOCK
        t = s0 + jnp.arange(Q_BLOCK)
        qa = lax.dynamic_slice_in_dim(q_a, s0, Q_BLOCK, axis=1)

        dist_c = t[:, None] - cmp_end[None, :]
        s_c = jnp.einsum('bqghd,bngd->bqghn', qa, kc) + bias_a[rel_bucket(dist_c)].transpose(0, 2, 3, 1)
        p_c = masked_softmax(s_c, (dist_c >= 0)[:, None, None, :])
        o_c = jnp.einsum('bqghn,bngd->bqghd', p_c.astype(vc.dtype), vc)

        imp = jnp.einsum('bqghn,nj->bqgj', p_c, overlap)
        cur = (t // SEL_BLOCK)[:, None]
        forced = (blk_j[None] == 0) | (blk_j[None] == cur) | (blk_j[None] == cur - 1)
        causal = blk_j[None] * SEL_BLOCK <= t[:, None]
        imp = jnp.where(forced[:, None, :], FORCE, jnp.where(causal[:, None, :], imp, -FORCE))
        _, idx = lax.top_k(imp, top_n)
        ks = ks_blk[b_ix, idx, :, g_ix, :].reshape(B, Q_BLOCK, G, top_n * SEL_BLOCK, HEAD_DIM)
        vs = vs_blk[b_ix, idx, :, g_ix, :].reshape(B, Q_BLOCK, G, top_n * SEL_BLOCK, HEAD_DIM)
        kpos = (idx[..., None] * SEL_BLOCK + jnp.arange(SEL_BLOCK)).reshape(B, Q_BLOCK, G, top_n * SEL_BLOCK)
        dist_s = t[None, :, None, None] - kpos
        s_s = (jnp.einsum('bqghd,bqgkd->bqghk', qa, ks)
               + jnp.moveaxis(bias_a[rel_bucket(dist_s), g_ix], -1, 3))
        p_s = masked_softmax(s_s, (dist_s >= 0)[:, :, :, None, :])
        o_s = jnp.einsum('bqghk,bqgkd->bqghd', p_s.astype(vs.dtype), vs)

        kw = lax.dynamic_slice_in_dim(k_w_pad, s0, NSA_WINDOW + Q_BLOCK, axis=1)
        vw = lax.dynamic_slice_in_dim(v_w_pad, s0, NSA_WINDOW + Q_BLOCK, axis=1)
        kwpos = s0 - NSA_WINDOW + jnp.arange(NSA_WINDOW + Q_BLOCK)
        dist_w = t[:, None] - kwpos[None, :]
        mask_w = (dist_w >= 0) & (dist_w < NSA_WINDOW) & (kwpos[None, :] >= 0)
        s_w = jnp.einsum('bqghd,bkgd->bqghk', qa, kw) + bias_a[rel_bucket(dist_w)].transpose(0, 2, 3, 1)
        p_w = masked_softmax(s_w, mask_w[:, None, None, :])
        o_w = jnp.einsum('bqghk,bkgd->bqghd', p_w.astype(vw.dtype), vw)

        g = lax.dynamic_slice_in_dim(g_nsa, s0, Q_BLOCK, axis=1)
        o_a = g[:, :, 0, :, :, None] * o_c + g[:, :, 1, :, :, None] * o_s + g[:, :, 2, :, :, None] * o_w

        qb = lax.dynamic_slice_in_dim(q_b, s0, Q_BLOCK, axis=1)
        kb = lax.dynamic_slice_in_dim(k_b_pad, s0, SWA_WINDOW + Q_BLOCK, axis=1)
        vb = lax.dynamic_slice_in_dim(v_b_pad, s0, SWA_WINDOW + Q_BLOCK, axis=1)
        kbpos = s0 - SWA_WINDOW + jnp.arange(SWA_WINDOW + Q_BLOCK)
        dist_b = t[:, None] - kbpos[None, :]
        mask_b = (dist_b >= 0) & (dist_b < SWA_WINDOW) & (kbpos[None, :] >= 0)
        s_b = jnp.einsum('bqnhd,bjnd->bqnhj', qb, kb) + bias_b[rel_bucket(dist_b)].transpose(0, 2, 3, 1)
        p_b = sink_softmax(s_b, mask_b[:, None, None, :], sinks)
        o_b = jnp.einsum('bqnhj,bjnd->bqnhd', p_b.astype(vb.dtype), vb)
        return o_a.reshape(B, Q_BLOCK, NSA_WIDTH), o_b.reshape(B, Q_BLOCK, SWA_WIDTH)

    o_a, o_b = lax.map(query_block, jnp.arange(S // Q_BLOCK))
    o_a = o_a.transpose(1, 0, 2, 3).reshape(B, S, NSA_WIDTH)
    o_b = o_b.transpose(1, 0, 2, 3).reshape(B, S, SWA_WIDTH)
    merged = jax.nn.sigmoid(gate_a) * (o_a @ w_up_a) + jax.nn.sigmoid(gate_b) * (o_b @ w_up_b)
    return merged @ w_out


def memory_xattn(h, m, w_xq, w_xkv, w_xo):
    B, S, _ = h.shape
    q = (h @ w_xq).reshape(B, S, XATTN_HEADS, XATTN_HEAD_DIM) * XATTN_HEAD_DIM ** -0.5
    k, v = jnp.split(m @ w_xkv, 2, axis=-1)
    k = k.reshape(B, -1, XATTN_HEADS, XATTN_HEAD_DIM)
    v = v.reshape(B, -1, XATTN_HEADS, XATTN_HEAD_DIM)
    p = jax.nn.softmax(jnp.einsum('bshd,bmhd->bhsm', q, k).astype(jnp.float32), axis=-1)
    o = jnp.einsum('bhsm,bmhd->bshd', p.astype(v.dtype), v)
    return o.reshape(B, S, D_MODEL) @ w_xo


def setup_inputs(seed: int = 0) -> dict:
    key = jax.random.key(seed)
    keys = iter(jax.random.split(key, 40))

    def w(shape, fan_in):
        return jax.random.normal(next(keys), shape, jnp.float32) * fan_in ** -0.5

    def gain(n=D_MODEL):
        return 1.0 + 0.01 * jax.random.normal(next(keys), (DEPTH, n), jnp.float32)

    L = DEPTH
    return {
        'x': jax.random.normal(next(keys), (BATCH, SEQ, D_MODEL), jnp.float32),
        'mem': jax.random.normal(next(keys), (BATCH, MEM_LEN, D_MODEL), jnp.float32),
        'norm_ffn1': gain(),
        'w1_gate': w((L, D_MODEL, D_FF), D_MODEL),
        'w1_up': w((L, D_MODEL, D_FF), D_MODEL),
        'w1_down': w((L, D_FF, D_MODEL), D_FF),
        'norm_mix': gain(),
        'w_in': w((L, D_MODEL, IN_WIDTH), D_MODEL),
        'cmp_pe_k': 0.1 * jax.random.normal(next(keys), (L, CMP_BLOCK, HEAD_DIM), jnp.float32),
        'cmp_w1_k': w((L, CMP_BLOCK * HEAD_DIM, CMP_HIDDEN), CMP_BLOCK * HEAD_DIM),
        'cmp_w2_k': w((L, CMP_HIDDEN, HEAD_DIM), CMP_HIDDEN),
        'cmp_pe_v': 0.1 * jax.random.normal(next(keys), (L, CMP_BLOCK, HEAD_DIM), jnp.float32),
        'cmp_w1_v': w((L, CMP_BLOCK * HEAD_DIM, CMP_HIDDEN), CMP_BLOCK * HEAD_DIM),
        'cmp_w2_v': w((L, CMP_HIDDEN, HEAD_DIM), CMP_HIDDEN),
        'attn_sinks': jax.random.normal(next(keys), (L, SWA_HEADS), jnp.float32),
        'rel_bias': 0.5 * jax.random.normal(next(keys), (REL_BUCKETS, NSA_HEADS + SWA_HEADS), jnp.float32),
        'w_up_a': w((L, NSA_WIDTH, D_MODEL), NSA_WIDTH),
        'w_up_b': w((L, SWA_WIDTH, D_MODEL), SWA_WIDTH),
        'w_out': w((L, D_MODEL, D_MODEL), D_MODEL),
        'norm_xattn': gain(),
        'norm_mem': gain(),
        'w_xq': w((L, D_MODEL, D_MODEL), D_MODEL),
        'w_xkv': w((L, D_MODEL, 2 * D_MODEL), D_MODEL),
        'w_xo': w((L, D_MODEL, D_MODEL), D_MODEL),
        'norm_ffn2': gain(),
        'w2_gate': w((L, D_MODEL, D_FF), D_MODEL),
        'w2_up': w((L, D_MODEL, D_FF), D_MODEL),
        'w2_down': w((L, D_FF, D_MODEL), D_FF),
        'norm_final': 1.0 + 0.01 * jax.random.normal(next(keys), (D_MODEL,), jnp.float32),
    }


def reference(x, mem, norm_ffn1, w1_gate, w1_up, w1_down, norm_mix, w_in,
              cmp_pe_k, cmp_w1_k, cmp_w2_k, cmp_pe_v, cmp_w1_v, cmp_w2_v,
              attn_sinks, rel_bias, w_up_a, w_up_b, w_out,
              norm_xattn, norm_mem, w_xq, w_xkv, w_xo,
              norm_ffn2, w2_gate, w2_up, w2_down, norm_final):
    for l in range(DEPTH):
        x = x + 0.5 * swiglu(rms_norm(x, norm_ffn1[l]), w1_gate[l], w1_up[l], w1_down[l])
        x = x + token_mixing(rms_norm(x, norm_mix[l]), w_in[l],
                             cmp_pe_k[l], cmp_w1_k[l], cmp_w2_k[l], cmp_pe_v[l], cmp_w1_v[l], cmp_w2_v[l],
                             attn_sinks[l], rel_bias, w_up_a[l], w_up_b[l], w_out[l])
        x = x + memory_xattn(rms_norm(x, norm_xattn[l]), rms_norm(mem, norm_mem[l]), w_xq[l], w_xkv[l], w_xo[l])
        x = x + 0.5 * swiglu(rms_norm(x, norm_ffn2[l]), w2_gate[l], w2_up[l], w2_down[l])
    return rms_norm(x, norm_final)
```

```python
import functools
import math

import numpy as np
import jax
import jax.numpy as jnp
from jax import lax
from jax.experimental import pallas as pl
from jax.experimental.pallas import tpu as pltpu

F32 = jnp.float32
BF16 = jnp.bfloat16

HEAD_DIM = 64
NSA_HEADS = 8
NSA_GROUPS = 2
HEADS_PER_GROUP = NSA_HEADS // NSA_GROUPS
CMP_BLOCK = 32
CMP_STRIDE = 16
CMP_HIDDEN = 256
SEL_BLOCK = 64
SEL_TOPN = 16
NSA_WINDOW = 512
SWA_HEADS = 8
SWA_KV_HEADS = 2
SWA_WINDOW = 128
REL_BUCKETS = 32
REL_MAX_DIST = 128
XATTN_HEADS = 4
EPS = 1e-6
NEG = -1e30

LANES = 128
TQ = 128
FAR_CHUNK = 512
NSA_WIN_CHUNKS = NSA_WINDOW // TQ + 1
SWA_WIN_CHUNKS = SWA_WINDOW // TQ + 1
VMEM_LIMIT = 56 * 1024 * 1024


def _rms(x, g):
    return x * lax.rsqrt(jnp.mean(x * x, axis=-1, keepdims=True) + EPS) * g


def _dot(a, b):
    return jnp.dot(a, b, preferred_element_type=F32)


def _dot_nt(a, b):
    return lax.dot_general(a, b, (((1,), (1,)), ((), ())), preferred_element_type=F32)


def _params(*sem):
    return pltpu.CompilerParams(dimension_semantics=sem, vmem_limit_bytes=VMEM_LIMIT)


def _ffn_kernel(x_ref, g_ref, wg_ref, wu_ref, wd_ref, gf_ref, o_ref, h_scr, acc_scr, *, final_norm):
    j = pl.program_id(1)

    @pl.when(j == 0)
    def _():
        h_scr[...] = _rms(x_ref[...], g_ref[...]).astype(BF16)
        acc_scr[...] = jnp.zeros_like(acc_scr)

    h = h_scr[...]
    a = _dot(h, wg_ref[...])
    u = _dot(h, wu_ref[...])
    act = a * jax.nn.sigmoid(a) * u
    acc_scr[...] += _dot(act.astype(BF16), wd_ref[...])

    @pl.when(j == pl.num_programs(1) - 1)
    def _():
        y = x_ref[...] + 0.5 * acc_scr[...]
        if final_norm:
            y = _rms(y, gf_ref[...])
        o_ref[...] = y


def _ffn(x2d, g, wg, wu, wd, gf, *, final_norm):
    n, d = x2d.shape
    dff = wg.shape[1]
    tm = min(1024, n)
    tf = 256
    return pl.pallas_call(
        functools.partial(_ffn_kernel, final_norm=final_norm),
        grid=(n // tm, dff // tf),
        in_specs=[
            pl.BlockSpec((tm, d), lambda i, j: (i, 0)),
            pl.BlockSpec((1, d), lambda i, j: (0, 0)),
            pl.BlockSpec((d, tf), lambda i, j: (0, j)),
            pl.BlockSpec((d, tf), lambda i, j: (0, j)),
            pl.BlockSpec((tf, d), lambda i, j: (j, 0)),
            pl.BlockSpec((1, d), lambda i, j: (0, 0)),
        ],
        out_specs=pl.BlockSpec((tm, d), lambda i, j: (i, 0)),
        out_shape=jax.ShapeDtypeStruct((n, d), F32),
        scratch_shapes=[pltpu.VMEM((tm, d), BF16), pltpu.VMEM((tm, d), F32)],
        compiler_params=_params("parallel", "arbitrary"),
        name="ffn",
    )(x2d, g, wg, wu, wd, gf)


def _inproj_kernel(x_ref, g_ref, wqa_ref, wqb_ref, wkv_ref, wc_ref, wg_ref,
                   qa_ref, qb_ref, kv_ref, kc_ref, vc_ref, gn_ref):
    h = _rms(x_ref[...], g_ref[...]).astype(BF16)
    qa_ref[...] = _dot(h, wqa_ref[...]).astype(BF16)
    qb_ref[...] = _dot(h, wqb_ref[...]).astype(BF16)
    kv_ref[...] = _dot(h, wkv_ref[...]).astype(BF16)
    c = _dot(h, wc_ref[...])
    kc_ref[...] = c[:, :LANES]
    vc_ref[...] = c[:, LANES:]
    gn_ref[...] = jax.nn.sigmoid(_dot(h, wg_ref[...]))


def _inproj(x2d, g, wqa, wqb, wkv, wc, wg):
    n, d = x2d.shape
    tm = min(512, n)
    full = lambda w: pl.BlockSpec(w.shape, lambda i: (0, 0))
    row = lambda width: pl.BlockSpec((tm, width), lambda i: (i, 0))
    return pl.pallas_call(
        _inproj_kernel,
        grid=(n // tm,),
        in_specs=[row(d), full(g), full(wqa), full(wqb), full(wkv), full(wc), full(wg)],
        out_specs=[row(wqa.shape[1]), row(wqb.shape[1]), row(wkv.shape[1]),
                   row(LANES), row(LANES), row(LANES)],
        out_shape=[
            jax.ShapeDtypeStruct((n, wqa.shape[1]), BF16),
            jax.ShapeDtypeStruct((n, wqb.shape[1]), BF16),
            jax.ShapeDtypeStruct((n, wkv.shape[1]), BF16),
            jax.ShapeDtypeStruct((n, LANES), F32),
            jax.ShapeDtypeStruct((n, LANES), F32),
            jax.ShapeDtypeStruct((n, LANES), F32),
        ],
        compiler_params=_params("parallel"),
        name="inproj",
    )(x2d, g, wqa, wqb, wkv, wc, wg)


def _compress_kernel(ak_ref, av_ref, pek_ref, pev_ref, wka_ref, wkb_ref, wva_ref, wvb_ref,
                     w2k_ref, w2v_ref, kc_ref, vc_ref):
    def one(a_ref, pe_ref, wa_ref, wb_ref, w2_ref, o_ref):
        a = a_ref[0]
        rows = a.shape[0]
        u = _dot((a + pe_ref[0:1, :]).astype(BF16), wa_ref[...])
        v = _dot((a + pe_ref[1:2, :]).astype(BF16), wb_ref[...])
        hid = u + pltpu.roll(v, rows - 1, 0)
        hid = jax.nn.gelu(hid, approximate=True)
        o_ref[0] = _dot(hid.astype(BF16), w2_ref[...]).astype(BF16)

    one(ak_ref, pek_ref, wka_ref, wkb_ref, w2k_ref, kc_ref)
    one(av_ref, pev_ref, wva_ref, wvb_ref, w2v_ref, vc_ref)


def _compress(ak, av, pek, pev, wka, wkb, wva, wvb, w2k, w2v):
    b, ncp, width = ak.shape
    full = lambda w: pl.BlockSpec(w.shape, lambda i: (0,) * w.ndim)
    per_b = lambda w: pl.BlockSpec((1, ncp, w), lambda i: (i, 0, 0))
    return pl.pallas_call(
        _compress_kernel,
        grid=(b,),
        in_specs=[per_b(width), per_b(width), full(pek), full(pev), full(wka), full(wkb),
                  full(wva), full(wvb), full(w2k), full(w2v)],
        out_specs=[per_b(LANES), per_b(LANES)],
        out_shape=[jax.ShapeDtypeStruct((b, ncp, LANES), BF16)] * 2,
        compiler_params=_params("parallel"),
        name="compress",
    )(ak, av, pek, pev, wka, wkb, wva, wvb, w2k, w2v)


def _stack_heads(q, heads):
    return jnp.concatenate([q[:, h * LANES:(h + 1) * LANES] for h in range(heads)], axis=0)


def _window_rows(ref, s0, window_chunks):
    parts = []
    for r in range(window_chunks):
        start = jnp.maximum(s0 + (r - (window_chunks - 1)) * TQ, 0)
        parts.append(ref[0, pl.ds(pl.multiple_of(start, TQ), TQ), :])
    return jnp.concatenate(parts, axis=0)


def _nsa_kernel(qa_ref, gn_ref, kc_ref, vc_ref, ks_ref, vs_ref, kw_ref, vw_ref,
                ov_ref, ef_ref, dc_ref, b31_ref, tw_ref, ts_ref,
                oa_ref, m_scr, l_scr, acc_scr, *, n_cmp, n_sel):
    c = pl.program_id(1)
    s0 = c * TQ
    rows = NSA_HEADS * TQ
    ncp = kc_ref.shape[1]
    nsp = ov_ref.shape[1]
    win = NSA_WIN_CHUNKS * TQ

    q = _stack_heads(qa_ref[0], NSA_HEADS)
    i_col = lax.broadcasted_iota(jnp.int32, (rows, 1), 0) % TQ
    t_col = s0 + i_col

    n_row = lax.broadcasted_iota(jnp.int32, (1, ncp), 1)
    m_idx = lax.broadcasted_iota(jnp.int32, (2 * CMP_BLOCK, ncp), 0) % CMP_BLOCK
    n_idx = lax.broadcasted_iota(jnp.int32, (2 * CMP_BLOCK, ncp), 1)
    place = jnp.where(n_idx == (s0 // CMP_STRIDE - CMP_STRIDE) + m_idx, 1.0, 0.0).astype(BF16)
    s_c = _dot_nt(q, kc_ref[0]) + b31_ref[...] + _dot(dc_ref[...], place)
    mask_c = (t_col >= n_row * CMP_STRIDE + (CMP_BLOCK - 1)) & (n_row < n_cmp)
    s_c = jnp.where(mask_c, s_c, NEG)
    e_c = jnp.where(mask_c, jnp.exp(s_c - jnp.max(s_c, axis=-1, keepdims=True)), 0.0)
    l_c = jnp.sum(e_c, axis=-1, keepdims=True)
    p_c = e_c / jnp.where(l_c == 0.0, 1.0, l_c)
    o_c = _dot(p_c.astype(BF16), vc_ref[0])

    t_q = s0 + lax.broadcasted_iota(jnp.int32, (TQ, 1), 0)
    cur_q = t_q // SEL_BLOCK
    blk_l = lax.broadcasted_iota(jnp.int32, (TQ, nsp), 1)
    forced = (blk_l == 0) | (blk_l == cur_q) | (blk_l == cur_q - 1)
    cand = (blk_l * SEL_BLOCK <= t_q) & jnp.logical_not(forced)
    t_lane = s0 + lax.broadcasted_iota(jnp.int32, (1, TQ), 1)
    cur_lane = t_lane // SEL_BLOCK
    n_forced = 1 + (cur_lane >= 1).astype(jnp.int32) + (cur_lane >= 2).astype(jnp.int32)
    k_free = min(SEL_TOPN, n_sel) - n_forced
    blk_s = lax.broadcasted_iota(jnp.int32, (nsp, TQ), 0).astype(F32)

    sel = []
    for g in range(NSA_GROUPS):
        base = g * HEADS_PER_GROUP * TQ
        psum = p_c[base:base + TQ]
        for h in range(1, HEADS_PER_GROUP):
            psum = psum + p_c[base + h * TQ:base + (h + 1) * TQ]
        hi = psum.astype(BF16)
        lo = (psum - hi.astype(F32)).astype(BF16)
        imp = _dot(hi, ov_ref[...]) + _dot(lo, ov_ref[...])
        v_t = jnp.where(cand, imp, -1.0).T
        sel_t = jnp.zeros((nsp, TQ), F32)
        for it in range(min(SEL_TOPN, n_sel) - 1):
            mx = jnp.max(v_t, axis=0, keepdims=True)
            first = jnp.min(jnp.where(v_t == mx, blk_s, float(nsp)), axis=0, keepdims=True)
            pick = (blk_s == first) & (it < k_free)
            sel_t = jnp.where(pick, 1.0, sel_t)
            v_t = jnp.where(pick, -1.0, v_t)
        sel.append(jnp.where(forced, 1.0, sel_t.T))

    j_row = lax.broadcasted_iota(jnp.int32, (1, win), 1)
    chunk_ok = j_row >= (NSA_WIN_CHUNKS - 1 - c) * TQ

    s_w = _dot_nt(q, _window_rows(kw_ref, s0, NSA_WIN_CHUNKS)) + tw_ref[...]
    s_w = jnp.where(chunk_ok, s_w, NEG)
    e_w = jnp.exp(s_w - jnp.max(s_w, axis=-1, keepdims=True))
    o_w = _dot(e_w.astype(BF16), _window_rows(vw_ref, s0, NSA_WIN_CHUNKS))
    o_w = o_w / jnp.sum(e_w, axis=-1, keepdims=True)

    blk_col = lax.broadcasted_iota(jnp.int32, (nsp, 1), 0)
    first_blk = (s0 - NSA_WINDOW) // SEL_BLOCK
    e_near = jnp.where(blk_col == first_blk + j_row // SEL_BLOCK, 1.0, 0.0).astype(BF16)

    def group_mask(sel_list, expand):
        return [_dot(s_.astype(BF16), expand) > 0.5 for s_ in sel_list]

    def apply_group_mask(s, masks):
        return jnp.concatenate(
            [jnp.where(masks[h // HEADS_PER_GROUP], s[h * TQ:(h + 1) * TQ], NEG)
             for h in range(NSA_HEADS)], axis=0)

    s_n = _dot_nt(q, _window_rows(ks_ref, s0, NSA_WIN_CHUNKS)) + ts_ref[...]
    s_n = jnp.where(chunk_ok, s_n, NEG)
    s_n = apply_group_mask(s_n, group_mask(sel, e_near))
    m_n = jnp.max(s_n, axis=-1, keepdims=True)
    e_n = jnp.exp(s_n - m_n)
    m_scr[...] = m_n
    l_scr[...] = jnp.sum(e_n, axis=-1, keepdims=True)
    acc_scr[...] = _dot(e_n.astype(BF16), _window_rows(vs_ref, s0, NSA_WIN_CHUNKS))

    blk_q = lax.broadcasted_iota(jnp.int32, (TQ, nsp), 1)
    sel_far = [jnp.where(blk_q < first_blk, s_, 0.0) for s_ in sel]
    n_far = jnp.maximum(s0 - NSA_WINDOW + FAR_CHUNK - 1, 0) // FAR_CHUNK
    b31 = b31_ref[...]

    def far_step(j, carry):
        k0 = pl.multiple_of(j * FAR_CHUNK, FAR_CHUNK)
        s_f = _dot_nt(q, ks_ref[0, pl.ds(k0, FAR_CHUNK), :]) + b31
        s_f = apply_group_mask(s_f, group_mask(sel_far, ef_ref[:, pl.ds(k0, FAR_CHUNK)]))
        m_old = m_scr[...]
        m_new = jnp.maximum(m_old, jnp.max(s_f, axis=-1, keepdims=True))
        alpha = jnp.exp(m_old - m_new)
        e_f = jnp.exp(s_f - m_new)
        l_scr[...] = alpha * l_scr[...] + jnp.sum(e_f, axis=-1, keepdims=True)
        acc_scr[...] = alpha * acc_scr[...] + _dot(e_f.astype(BF16), vs_ref[0, pl.ds(k0, FAR_CHUNK), :])
        m_scr[...] = m_new
        return carry

    lax.fori_loop(0, n_far, far_step, 0)
    o_s = acc_scr[...] / l_scr[...]

    gn = gn_ref[0]
    for h in range(NSA_HEADS):
        r = slice(h * TQ, (h + 1) * TQ)
        o = (gn[:, h:h + 1] * o_c[r] + gn[:, NSA_HEADS + h:NSA_HEADS + h + 1] * o_s[r]
             + gn[:, 2 * NSA_HEADS + h:2 * NSA_HEADS + h + 1] * o_w[r])
        oa_ref[0, :, h * LANES:(h + 1) * LANES] = o.astype(BF16)


def _nsa(qa, gn, kc, vc, kv, ov, ef, dc, b31, tw, ts, *, n_cmp, n_sel):
    b, s, qw = qa.shape
    ncp = kc.shape[1]
    rows = NSA_HEADS * TQ
    full = lambda w: pl.BlockSpec(w.shape, lambda i, j: (0,) * w.ndim)
    kv_spec = lambda col: pl.BlockSpec((1, s, LANES), lambda i, j: (i, 0, col))
    return pl.pallas_call(
        functools.partial(_nsa_kernel, n_cmp=n_cmp, n_sel=n_sel),
        grid=(b, s // TQ),
        in_specs=[
            pl.BlockSpec((1, TQ, qw), lambda i, j: (i, j, 0)),
            pl.BlockSpec((1, TQ, LANES), lambda i, j: (i, j, 0)),
            pl.BlockSpec((1, ncp, LANES), lambda i, j: (i, 0, 0)),
            pl.BlockSpec((1, ncp, LANES), lambda i, j: (i, 0, 0)),
            kv_spec(0), kv_spec(1), kv_spec(2), kv_spec(3),
            full(ov), full(ef), full(dc), full(b31), full(tw), full(ts),
        ],
        out_specs=pl.BlockSpec((1, TQ, qw), lambda i, j: (i, j, 0)),
        out_shape=jax.ShapeDtypeStruct((b, s, qw), BF16),
        scratch_shapes=[pltpu.VMEM((rows, 1), F32), pltpu.VMEM((rows, 1), F32),
                        pltpu.VMEM((rows, LANES), F32)],
        compiler_params=_params("parallel", "arbitrary"),
        name="nsa",
    )(qa, gn, kc, vc, kv, kv, kv, kv, ov, ef, dc, b31, tw, ts)


def _swa_kernel(qb_ref, kb_ref, vb_ref, tb_ref, sink_ref, ob_ref):
    c = pl.program_id(1)
    s0 = c * TQ
    win = SWA_WIN_CHUNKS * TQ
    q = _stack_heads(qb_ref[0], SWA_HEADS)
    j_row = lax.broadcasted_iota(jnp.int32, (1, win), 1)
    chunk_ok = j_row >= (SWA_WIN_CHUNKS - 1 - c) * TQ
    s = _dot_nt(q, _window_rows(kb_ref, s0, SWA_WIN_CHUNKS)) + tb_ref[...]
    s = jnp.where(chunk_ok, s, NEG)
    sink = sink_ref[...]
    m = jnp.maximum(jnp.max(s, axis=-1, keepdims=True), sink)
    e = jnp.exp(s - m)
    l = jnp.sum(e, axis=-1, keepdims=True) + jnp.exp(sink - m)
    o = _dot(e.astype(BF16), _window_rows(vb_ref, s0, SWA_WIN_CHUNKS)) / l
    for h in range(SWA_HEADS):
        ob_ref[0, :, h * LANES:(h + 1) * LANES] = o[h * TQ:(h + 1) * TQ].astype(BF16)


def _swa(qb, kv, tb, sink):
    b, s, qw = qb.shape
    full = lambda w: pl.BlockSpec(w.shape, lambda i, j: (0,) * w.ndim)
    kv_spec = lambda col: pl.BlockSpec((1, s, LANES), lambda i, j: (i, 0, col))
    return pl.pallas_call(
        _swa_kernel,
        grid=(b, s // TQ),
        in_specs=[pl.BlockSpec((1, TQ, qw), lambda i, j: (i, j, 0)), kv_spec(4), kv_spec(5),
                  full(tb), full(sink)],
        out_specs=pl.BlockSpec((1, TQ, qw), lambda i, j: (i, j, 0)),
        out_shape=jax.ShapeDtypeStruct((b, s, qw), BF16),
        compiler_params=_params("parallel", "arbitrary"),
        name="swa",
    )(qb, kv, kv, tb, sink)


def _merge_kernel(x_ref, g_ref, oa_ref, ob_ref, wga_ref, wgb_ref, wua_ref, wub_ref, wo_ref, o_ref):
    x = x_ref[...]
    h = _rms(x, g_ref[...]).astype(BF16)
    merged = (jax.nn.sigmoid(_dot(h, wga_ref[...])) * _dot(oa_ref[...], wua_ref[...])
              + jax.nn.sigmoid(_dot(h, wgb_ref[...])) * _dot(ob_ref[...], wub_ref[...]))
    o_ref[...] = x + _dot(merged.astype(BF16), wo_ref[...])


def _merge(x2d, g, oa, ob, wga, wgb, wua, wub, wo):
    n, d = x2d.shape
    tm = min(512, n)
    full = lambda w: pl.BlockSpec(w.shape, lambda i: (0, 0))
    row = lambda width: pl.BlockSpec((tm, width), lambda i: (i, 0))
    return pl.pallas_call(
        _merge_kernel,
        grid=(n // tm,),
        in_specs=[row(d), full(g), row(oa.shape[1]), row(ob.shape[1]), full(wga), full(wgb),
                  full(wua), full(wub), full(wo)],
        out_specs=row(d),
        out_shape=jax.ShapeDtypeStruct((n, d), F32),
        compiler_params=_params("parallel"),
        name="merge",
    )(x2d, g, oa, ob, wga, wgb, wua, wub, wo)


def _memkv_kernel(m_ref, g_ref, w_ref, o_ref):
    o_ref[...] = _dot(_rms(m_ref[...], g_ref[...]).astype(BF16), w_ref[...]).astype(BF16)


def _memkv(m2d, g, w):
    n, d = m2d.shape
    tm = min(512, n)
    return pl.pallas_call(
        _memkv_kernel,
        grid=(n // tm,),
        in_specs=[pl.BlockSpec((tm, d), lambda i: (i, 0)), pl.BlockSpec(g.shape, lambda i: (0, 0)),
                  pl.BlockSpec(w.shape, lambda i: (0, 0))],
        out_specs=pl.BlockSpec((tm, w.shape[1]), lambda i: (i, 0)),
        out_shape=jax.ShapeDtypeStruct((n, w.shape[1]), BF16),
        compiler_params=_params("parallel"),
        name="memkv",
    )(m2d, g, w)


def _xattn_kernel(x_ref, g_ref, kv_ref, wq_ref, wo_ref, o_ref):
    x = x_ref[0]
    d = x.shape[1]
    hd = d // XATTN_HEADS
    q = _dot(_rms(x, g_ref[...]).astype(BF16), wq_ref[...]).astype(BF16)
    outs = []
    for h in range(XATTN_HEADS):
        k = kv_ref[0, :, h * hd:(h + 1) * hd]
        v = kv_ref[0, :, d + h * hd:d + (h + 1) * hd]
        s = _dot_nt(q[:, h * hd:(h + 1) * hd], k)
        e = jnp.exp(s - jnp.max(s, axis=-1, keepdims=True))
        outs.append((_dot(e.astype(BF16), v) / jnp.sum(e, axis=-1, keepdims=True)).astype(BF16))
    o_ref[0] = x + _dot(jnp.concatenate(outs, axis=1), wo_ref[...])


def _xattn(x3d, g, kv, wq, wo):
    b, s, d = x3d.shape
    tm = min(512, s)
    full = lambda w: pl.BlockSpec(w.shape, lambda i, j: (0, 0))
    return pl.pallas_call(
        _xattn_kernel,
        grid=(b, s // tm),
        in_specs=[pl.BlockSpec((1, tm, d), lambda i, j: (i, j, 0)), full(g),
                  pl.BlockSpec((1,) + kv.shape[1:], lambda i, j: (i, 0, 0)), full(wq), full(wo)],
        out_specs=pl.BlockSpec((1, tm, d), lambda i, j: (i, j, 0)),
        out_shape=jax.ShapeDtypeStruct((b, s, d), F32),
        compiler_params=_params("parallel", "parallel"),
        name="xattn",
    )(x3d, g, kv, wq, wo)


def _rel_bucket(dist):
    dist = jnp.maximum(dist, 0)
    max_exact = REL_BUCKETS // 2
    d = jnp.maximum(dist, 1).astype(F32)
    large = max_exact + (jnp.log(d / max_exact) / math.log(REL_MAX_DIST / max_exact)
                         * (REL_BUCKETS - max_exact)).astype(jnp.int32)
    large = jnp.minimum(large, REL_BUCKETS - 1)
    return jnp.where(dist < max_exact, dist, large)


def _expand_heads(w, heads, kv_heads):
    d = w.shape[0]
    group = np.arange(heads) // (heads // kv_heads)
    onehot = jnp.asarray(np.eye(kv_heads, dtype=np.float32)[group])
    w4 = w.reshape(d, heads, 1, HEAD_DIM) * onehot[None, :, :, None]
    return w4.reshape(d, heads * kv_heads * HEAD_DIM)


def _expand_heads_rows(w, heads, kv_heads):
    return _expand_heads(w.T, heads, kv_heads).T


def _bias_tile(bias, window_chunks, lo, hi):
    win = window_chunks * TQ
    i = jnp.arange(TQ)[:, None]
    j = jnp.arange(win)[None, :]
    dist = i + (window_chunks - 1) * TQ - j
    vals = bias[_rel_bucket(dist)]
    ok = ((dist >= lo) & (dist < hi))[:, :, None]
    return jnp.where(ok, vals, NEG).transpose(2, 0, 1).reshape(-1, win)


def _cmp_tables(w1, pe):
    half = CMP_BLOCK // 2
    eye = jnp.eye(NSA_GROUPS, dtype=F32)
    w = w1.reshape(2, half, HEAD_DIM, CMP_HIDDEN)
    wexp = jnp.einsum('aldh,gk->algdkh', w, eye)
    wexp = wexp.reshape(2, half * NSA_GROUPS * HEAD_DIM, NSA_GROUPS * CMP_HIDDEN)
    pexp = jnp.broadcast_to(pe.reshape(2, half, 1, HEAD_DIM), (2, half, NSA_GROUPS, HEAD_DIM))
    return wexp[0].astype(BF16), wexp[1].astype(BF16), pexp.reshape(2, -1)


def _block_diag2(w):
    z = jnp.zeros_like(w)
    return jnp.concatenate([jnp.concatenate([w, z], axis=1), jnp.concatenate([z, w], axis=1)], axis=0)


def kernel(x, mem, norm_ffn1, w1_gate, w1_up, w1_down, norm_mix, w_in, cmp_pe_k, cmp_w1_k, cmp_w2_k,
           cmp_pe_v, cmp_w1_v, cmp_w2_v, attn_sinks, rel_bias, w_up_a, w_up_b, w_out, norm_xattn,
           norm_mem, w_xq, w_xkv, w_xo, norm_ffn2, w2_gate, w2_up, w2_down, norm_final):
    b, s, d = x.shape
    n = b * s
    assert norm_ffn1.shape[0] == 1, "single-layer kernel"
    assert s % FAR_CHUNK == 0 and s >= NSA_WINDOW + TQ
    n_cmp = (s - CMP_BLOCK) // CMP_STRIDE + 1
    ncp = s // CMP_STRIDE
    n_sel = s // SEL_BLOCK
    nsp = max(LANES, n_sel)
    bf = lambda w: w.astype(BF16)
    scale = HEAD_DIM ** -0.5

    wi = w_in[0]
    nq = NSA_HEADS * HEAD_DIM
    nkv = NSA_GROUPS * HEAD_DIM
    o_g = nq + 6 * nkv
    o_qb = o_g + 3 * NSA_HEADS
    o_kb = o_qb + SWA_HEADS * HEAD_DIM
    o_ga = o_kb + 2 * SWA_KV_HEADS * HEAD_DIM
    wqa = bf(_expand_heads(wi[:, :nq] * scale, NSA_HEADS, NSA_GROUPS))
    wqb = bf(_expand_heads(wi[:, o_qb:o_kb] * scale, SWA_HEADS, SWA_KV_HEADS))
    wkv = bf(jnp.concatenate([wi[:, nq + 2 * nkv:o_g], wi[:, o_kb:o_ga]], axis=1))
    wc = bf(wi[:, nq:nq + 2 * nkv])
    wgn = bf(jnp.pad(wi[:, o_g:o_qb], ((0, 0), (0, LANES - 3 * NSA_HEADS))))
    wga = bf(wi[:, o_ga:o_ga + d])
    wgb = bf(wi[:, o_ga + d:o_ga + 2 * d])
    wua = bf(_expand_heads_rows(w_up_a[0], NSA_HEADS, NSA_GROUPS))
    wub = bf(_expand_heads_rows(w_up_b[0], SWA_HEADS, SWA_KV_HEADS))
    wka, wkb, pek = _cmp_tables(cmp_w1_k[0], cmp_pe_k[0])
    wva, wvb, pev = _cmp_tables(cmp_w1_v[0], cmp_pe_v[0])
    w2k = bf(_block_diag2(cmp_w2_k[0]))
    w2v = bf(_block_diag2(cmp_w2_v[0]))

    bias_a = rel_bias[:, :NSA_HEADS]
    bias_b = rel_bias[:, NSA_HEADS:]
    tw = _bias_tile(bias_a, NSA_WIN_CHUNKS, 0, NSA_WINDOW)
    ts = _bias_tile(bias_a, NSA_WIN_CHUNKS, 0, NSA_WIN_CHUNKS * TQ)
    tb = _bias_tile(bias_b, SWA_WIN_CHUNKS, 0, SWA_WINDOW)
    far_bias = bias_a[_rel_bucket(jnp.int32(REL_MAX_DIST))]
    b31 = jnp.repeat(far_bias, TQ)[:, None]
    i = jnp.arange(TQ)[:, None]
    m = jnp.arange(CMP_BLOCK)[None, :]
    dist_c = i + (CMP_STRIDE * CMP_STRIDE - (CMP_BLOCK - 1)) - CMP_STRIDE * m
    near = ((dist_c >= 0) & (dist_c < REL_MAX_DIST))[:, :, None]
    dcv = jnp.where(near, bias_a[_rel_bucket(dist_c)] - far_bias[None, None, :], 0.0)
    dcv = dcv.transpose(2, 0, 1).reshape(NSA_HEADS * TQ, CMP_BLOCK)
    dc_hi = dcv.astype(BF16)
    dc = jnp.concatenate([dc_hi, (dcv - dc_hi.astype(F32)).astype(BF16)], axis=1)
    sink = jnp.repeat(attn_sinks[0], TQ)[:, None]

    cs = np.arange(ncp)[:, None] * CMP_STRIDE
    ss = np.arange(nsp)[None, :] * SEL_BLOCK
    ovl = np.maximum(np.minimum(cs + CMP_BLOCK, ss + SEL_BLOCK) - np.maximum(cs, ss), 0) / CMP_BLOCK
    ovl[n_cmp:] = 0.0
    ov = jnp.asarray(ovl, dtype=BF16)
    ef = jnp.asarray(np.arange(nsp)[:, None] == (np.arange(s)[None, :] // SEL_BLOCK), dtype=BF16)

    row = lambda g: g.reshape(1, -1)
    x1 = _ffn(x.reshape(n, d), row(norm_ffn1[0]), bf(w1_gate[0]), bf(w1_up[0]), bf(w1_down[0]),
              row(norm_final), final_norm=False)
    qa, qb, kv, kcf, vcf, gn = _inproj(x1, row(norm_mix[0]), wqa, wqb, wkv, wc, wgn)
    tok_w = CMP_STRIDE * LANES
    kc, vc = _compress(kcf.reshape(b, ncp, tok_w), vcf.reshape(b, ncp, tok_w), pek, pev,
                       wka, wkb, wva, wvb, w2k, w2v)
    kv3 = kv.reshape(b, s, -1)
    oa = _nsa(qa.reshape(b, s, -1), gn.reshape(b, s, LANES), kc, vc, kv3, ov, ef, dc, b31, tw, ts,
              n_cmp=n_cmp, n_sel=n_sel)
    ob = _swa(qb.reshape(b, s, -1), kv3, tb, sink)
    x2 = _merge(x1, row(norm_mix[0]), oa.reshape(n, -1), ob.reshape(n, -1), wga, wgb, wua, wub,
                bf(w_out[0]))
    mkv = _memkv(mem.reshape(-1, d), row(norm_mem[0]), bf(w_xkv[0]))
    x3 = _xattn(x2.reshape(b, s, d), row(norm_xattn[0]), mkv.reshape(b, -1, 2 * d),
                bf(w_xq[0] * (d // XATTN_HEADS) ** -0.5), bf(w_xo[0]))
    out = _ffn(x3.reshape(n, d), row(norm_ffn2[0]), bf(w2_gate[0]), bf(w2_up[0]), bf(w2_down[0]),
               row(norm_final), final_norm=True)
    return out.reshape(b, s, d)
```

```python
import functools
import math

import numpy as np
import jax
import jax.numpy as jnp
from jax import lax
from jax.experimental import pallas as pl
from jax.experimental.pallas import tpu as pltpu

F32 = jnp.float32
BF16 = jnp.bfloat16

HEAD_DIM = 64
NSA_HEADS = 8
NSA_GROUPS = 2
HEADS_PER_GROUP = NSA_HEADS // NSA_GROUPS
CMP_BLOCK = 32
CMP_STRIDE = 16
CMP_HIDDEN = 256
SEL_BLOCK = 64
SEL_TOPN = 16
NSA_WINDOW = 512
SWA_HEADS = 8
SWA_KV_HEADS = 2
SWA_WINDOW = 128
REL_BUCKETS = 32
REL_MAX_DIST = 128
XATTN_HEADS = 4
EPS = 1e-6
NEG = -1e30

LANES = 128
TQ = 128
FAR_CHUNK = 512
NSA_WIN_CHUNKS = NSA_WINDOW // TQ + 1
SWA_WIN_CHUNKS = SWA_WINDOW // TQ + 1
CMP_WIN_BACK = 16
VMEM_LIMIT = 56 * 1024 * 1024


def _rms(x, g):
    return x * lax.rsqrt(jnp.mean(x * x, axis=-1, keepdims=True) + EPS) * g


def _dot(a, b):
    return jnp.dot(a, b, preferred_element_type=F32)


def _dot_nt(a, b):
    return lax.dot_general(a, b, (((1,), (1,)), ((), ())), preferred_element_type=F32)


def _params(*sem):
    return pltpu.CompilerParams(dimension_semantics=sem, vmem_limit_bytes=VMEM_LIMIT)


def _ffn_kernel(x_ref, g_ref, wg_ref, wu_ref, wd_ref, gf_ref, o_ref, h_scr, acc_scr, *, final_norm):
    j = pl.program_id(1)

    @pl.when(j == 0)
    def _():
        h_scr[...] = _rms(x_ref[...], g_ref[...]).astype(BF16)
        acc_scr[...] = jnp.zeros_like(acc_scr)

    h = h_scr[...]
    a = _dot(h, wg_ref[...])
    u = _dot(h, wu_ref[...])
    act = a * jax.nn.sigmoid(a) * u
    acc_scr[...] += _dot(act.astype(BF16), wd_ref[...])

    @pl.when(j == pl.num_programs(1) - 1)
    def _():
        y = x_ref[...] + 0.5 * acc_scr[...]
        if final_norm:
            y = _rms(y, gf_ref[...])
        o_ref[...] = y


def _ffn(x2d, g, wg, wu, wd, gf, *, final_norm):
    n, d = x2d.shape
    dff = wg.shape[1]
    tm = min(1024, n)
    tf = 256
    return pl.pallas_call(
        functools.partial(_ffn_kernel, final_norm=final_norm),
        grid=(n // tm, dff // tf),
        in_specs=[
            pl.BlockSpec((tm, d), lambda i, j: (i, 0)),
            pl.BlockSpec((1, d), lambda i, j: (0, 0)),
            pl.BlockSpec((d, tf), lambda i, j: (0, j)),
            pl.BlockSpec((d, tf), lambda i, j: (0, j)),
            pl.BlockSpec((tf, d), lambda i, j: (j, 0)),
            pl.BlockSpec((1, d), lambda i, j: (0, 0)),
        ],
        out_specs=pl.BlockSpec((tm, d), lambda i, j: (i, 0)),
        out_shape=jax.ShapeDtypeStruct((n, d), F32),
        scratch_shapes=[pltpu.VMEM((tm, d), BF16), pltpu.VMEM((tm, d), F32)],
        compiler_params=_params("parallel", "arbitrary"),
        name="ffn",
    )(x2d, g, wg, wu, wd, gf)


def _inproj_kernel(x_ref, g_ref, wqa_ref, wqb_ref, wkv_ref, wc_ref, wg_ref,
                   qa_ref, qb_ref, kv_ref, kc_ref, vc_ref, gn_ref, *, seq_len):
    tm = x_ref.shape[0]
    h = _rms(x_ref[...], g_ref[...]).astype(BF16)
    qa_ref[...] = _dot(h, wqa_ref[...]).astype(BF16)
    qb_ref[...] = _dot(h, wqb_ref[...]).astype(BF16)
    kv = _dot(h, wkv_ref[...]).astype(BF16)
    pos = (pl.program_id(0) * tm + lax.broadcasted_iota(jnp.int32, (tm, LANES), 0)) % seq_len
    lane = lax.broadcasted_iota(jnp.int32, (tm, LANES), 1)
    kv_ref[:, 0:LANES] = kv[:, 0:LANES]
    kv_ref[:, LANES:2 * LANES] = jnp.where(lane == pos // SEL_BLOCK, 1.0, 0.0).astype(BF16)
    kv_ref[:, 2 * LANES:3 * LANES] = kv[:, LANES:2 * LANES]
    kv_ref[:, 3 * LANES:4 * LANES] = jnp.ones((tm, LANES), BF16)
    kv_ref[:, 4 * LANES:] = kv[:, 2 * LANES:]
    c = _dot(h, wc_ref[...])
    kc_ref[...] = c[:, :LANES]
    vc_ref[...] = c[:, LANES:]
    gn_ref[...] = jax.nn.sigmoid(_dot(h, wg_ref[...]))


def _inproj(x2d, g, wqa, wqb, wkv, wc, wg, *, seq_len):
    n, d = x2d.shape
    kv_w = wkv.shape[1] + 2 * LANES
    tm = min(512, n)
    full = lambda w: pl.BlockSpec(w.shape, lambda i: (0, 0))
    row = lambda width: pl.BlockSpec((tm, width), lambda i: (i, 0))
    return pl.pallas_call(
        functools.partial(_inproj_kernel, seq_len=seq_len),
        grid=(n // tm,),
        in_specs=[row(d), full(g), full(wqa), full(wqb), full(wkv), full(wc), full(wg)],
        out_specs=[row(wqa.shape[1]), row(wqb.shape[1]), row(kv_w), row(LANES), row(LANES), row(LANES)],
        out_shape=[
            jax.ShapeDtypeStruct((n, wqa.shape[1]), BF16),
            jax.ShapeDtypeStruct((n, wqb.shape[1]), BF16),
            jax.ShapeDtypeStruct((n, kv_w), BF16),
            jax.ShapeDtypeStruct((n, LANES), F32),
            jax.ShapeDtypeStruct((n, LANES), F32),
            jax.ShapeDtypeStruct((n, LANES), F32),
        ],
        compiler_params=_params("parallel"),
        name="inproj",
    )(x2d, g, wqa, wqb, wkv, wc, wg)


def _compress_kernel(ak_ref, av_ref, pek_ref, pev_ref, wka_ref, wkb_ref, wva_ref, wvb_ref,
                     w2k_ref, w2v_ref, kc_ref, vc_ref):
    def one(a_ref, pe_ref, wa_ref, wb_ref, w2_ref, o_ref):
        a = a_ref[0]
        rows = a.shape[0]
        u = _dot((a + pe_ref[0:1, :]).astype(BF16), wa_ref[...])
        v = _dot((a + pe_ref[1:2, :]).astype(BF16), wb_ref[...])
        hid = u + pltpu.roll(v, rows - 1, 0)
        hid = jax.nn.gelu(hid, approximate=True)
        o_ref[0] = _dot(hid.astype(BF16), w2_ref[...]).astype(BF16)

    one(ak_ref, pek_ref, wka_ref, wkb_ref, w2k_ref, kc_ref)
    one(av_ref, pev_ref, wva_ref, wvb_ref, w2v_ref, vc_ref)


def _compress(ak, av, pek, pev, wka, wkb, wva, wvb, w2k, w2v):
    b, ncp, width = ak.shape
    full = lambda w: pl.BlockSpec(w.shape, lambda i: (0,) * w.ndim)
    per_b = lambda w: pl.BlockSpec((1, ncp, w), lambda i: (i, 0, 0))
    return pl.pallas_call(
        _compress_kernel,
        grid=(b,),
        in_specs=[per_b(width), per_b(width), full(pek), full(pev), full(wka), full(wkb),
                  full(wva), full(wvb), full(w2k), full(w2v)],
        out_specs=[per_b(LANES), per_b(LANES)],
        out_shape=[jax.ShapeDtypeStruct((b, ncp, LANES), BF16)] * 2,
        compiler_params=_params("parallel"),
        name="compress",
    )(ak, av, pek, pev, wka, wkb, wva, wvb, w2k, w2v)


def _stack_heads(q, heads):
    return jnp.concatenate([q[:, h * LANES:(h + 1) * LANES] for h in range(heads)], axis=0)


def _window_rows(ref, s0, window_chunks, fallback=0):
    parts = []
    for r in range(window_chunks):
        start = s0 + (r - (window_chunks - 1)) * TQ
        start = jnp.where(start < 0, fallback, start)
        parts.append(ref[0, pl.ds(pl.multiple_of(start, TQ), TQ), :])
    return jnp.concatenate(parts, axis=0)


def _lanes(a, b):
    return jnp.concatenate([a, b], axis=1)


def _per_head(tiles_per_group):
    return jnp.concatenate([tiles_per_group[h // HEADS_PER_GROUP] for h in range(NSA_HEADS)], axis=0)


def _mix_kernel(qa_ref, qb_ref, gn_ref, kc_ref, vc_ref, ks_ref, vs_ref, kw_ref, vw_ref, kb_ref, vb_ref,
                ov_ref, ca_ref, tw_ref, ts_ref, tb_ref, sink_ref,
                oa_ref, ob_ref, m_scr, acc_scr, sfa_scr, sfb_scr, *, n_sel):
    c = pl.program_id(1)
    s0 = c * TQ
    rows = NSA_HEADS * TQ
    ncp = kc_ref.shape[1]
    nsp = ov_ref.shape[1]
    win = NSA_WIN_CHUNKS * TQ
    ones_v = lambda n: jnp.ones((n, LANES), BF16)
    neg_q = jnp.full((rows, LANES), NEG, BF16)

    q = _stack_heads(qa_ref[0], NSA_HEADS)
    t_col = s0 + lax.broadcasted_iota(jnp.int32, (rows, 1), 0) % TQ

    n_i = lax.broadcasted_iota(jnp.int32, (ncp, LANES), 0)
    lane_c = lax.broadcasted_iota(jnp.int32, (ncp, LANES), 1)
    w0 = s0 // CMP_STRIDE - CMP_WIN_BACK
    slot = jnp.where(lane_c < CMP_BLOCK, lane_c, lane_c - CMP_BLOCK)
    place = ((lane_c < 2 * CMP_BLOCK) & (n_i == w0 + slot)) | (
        (lane_c == 2 * CMP_BLOCK) & (n_i >= w0 + CMP_BLOCK))
    kc_aug = _lanes(kc_ref[0], jnp.where(place, 1.0, 0.0).astype(BF16))
    s_c = _dot_nt(_lanes(q, ca_ref[...]), kc_aug)
    e_c = jnp.exp(s_c - jnp.max(s_c, axis=-1, keepdims=True))
    l_c = jnp.sum(e_c, axis=-1, keepdims=True)
    p_c = e_c * jnp.where(t_col >= CMP_BLOCK - 1, 1.0 / l_c, 0.0)
    o_c = _dot(p_c.astype(BF16), vc_ref[0])

    t_q = s0 + lax.broadcasted_iota(jnp.int32, (TQ, 1), 0)
    cur_q = t_q // SEL_BLOCK
    blk_l = lax.broadcasted_iota(jnp.int32, (TQ, nsp), 1)
    forced = (blk_l == 0) | (blk_l == cur_q) | (blk_l == cur_q - 1)
    causal_blk = blk_l * SEL_BLOCK <= t_q
    cand = causal_blk & jnp.logical_not(forced)
    t_lane = s0 + lax.broadcasted_iota(jnp.int32, (1, TQ), 1)
    cur_lane = t_lane // SEL_BLOCK
    n_forced = 1 + (cur_lane >= 1).astype(jnp.int32) + (cur_lane >= 2).astype(jnp.int32)
    k_free = min(SEL_TOPN, n_sel) - n_forced
    blk_s = lax.broadcasted_iota(jnp.int32, (nsp, TQ), 0).astype(F32)
    first_blk = (s0 - NSA_WINDOW) // SEL_BLOCK

    near_q, far_q = [], []
    for g in range(NSA_GROUPS):
        base = g * HEADS_PER_GROUP * TQ
        psum = p_c[base:base + TQ]
        for h in range(1, HEADS_PER_GROUP):
            psum = psum + p_c[base + h * TQ:base + (h + 1) * TQ]
        hi = psum.astype(BF16)
        lo = (psum - hi.astype(F32)).astype(BF16)
        imp = _dot(hi, ov_ref[...]) + _dot(lo, ov_ref[...])
        v_t = jnp.where(cand, imp, -1.0).T
        sel_t = jnp.zeros((nsp, TQ), F32)
        for it in range(min(SEL_TOPN, n_sel) - 1):
            mx = jnp.max(v_t, axis=0, keepdims=True)
            first = jnp.min(jnp.where(v_t == mx, blk_s, float(nsp)), axis=0, keepdims=True)
            pick = (blk_s == first) & (it < k_free)
            sel_t = jnp.where(pick, 1.0, sel_t)
            v_t = jnp.where(pick, -1.0, v_t)
        chosen = (forced | (sel_t.T > 0.5)) & causal_blk
        near_q.append(jnp.where(chosen, 0.0, NEG).astype(BF16))
        far_q.append(jnp.where(chosen & (blk_l < first_blk), 0.0, NEG).astype(BF16))

    j_w = lax.broadcasted_iota(jnp.int32, (win, LANES), 0)
    lane_w = lax.broadcasted_iota(jnp.int32, (win, LANES), 1)
    before_start = j_w < (NSA_WIN_CHUNKS - 1 - c) * TQ
    invalid_w = jnp.where((lane_w == 0) & before_start, 1.0, 0.0).astype(BF16)

    s_w = _dot_nt(_lanes(q, neg_q), _lanes(_window_rows(kw_ref, s0, NSA_WIN_CHUNKS), invalid_w))
    s_w = s_w + tw_ref[...]
    e_w = jnp.exp(s_w - jnp.max(s_w, axis=-1, keepdims=True)).astype(BF16)
    r_w = _dot(e_w, _lanes(_window_rows(vw_ref, s0, NSA_WIN_CHUNKS), ones_v(win)))
    o_w = r_w[:, :LANES] / r_w[:, LANES:]

    s_len = ks_ref.shape[1]
    s_n = _dot_nt(_lanes(q, _per_head(near_q)),
                  _window_rows(ks_ref, s0, NSA_WIN_CHUNKS, s_len - TQ)) + ts_ref[...]
    m_n = jnp.max(s_n, axis=-1, keepdims=True)
    m_scr[...] = m_n
    acc_scr[...] = _dot(jnp.exp(s_n - m_n).astype(BF16),
                        _window_rows(vs_ref, s0, NSA_WIN_CHUNKS, s_len - TQ))

    q_far = _lanes(q, _per_head(far_q))
    n_far = jnp.maximum(s0 - NSA_WINDOW + FAR_CHUNK - 1, 0) // FAR_CHUNK
    last_chunk = s_len // FAR_CHUNK - 1

    def far_rows(j):
        return pl.ds(pl.multiple_of(jnp.minimum(j, last_chunk) * FAR_CHUNK, FAR_CHUNK), FAR_CHUNK)

    def far_logits(j):
        return _dot_nt(q_far, ks_ref[0, far_rows(j), :])

    def far_update(s_f, j):
        m_old = m_scr[...]
        m_new = jnp.maximum(m_old, jnp.max(s_f, axis=-1, keepdims=True))
        e_f = jnp.exp(s_f - m_new).astype(BF16)
        acc_scr[...] = jnp.exp(m_old - m_new) * acc_scr[...] + _dot(e_f, vs_ref[0, far_rows(j), :])
        m_scr[...] = m_new

    sfa_scr[...] = far_logits(0)

    def far_step(i, carry):
        sfb_scr[...] = far_logits(2 * i + 1)
        far_update(sfa_scr[...], 2 * i)
        sfa_scr[...] = far_logits(2 * i + 2)
        far_update(sfb_scr[...], 2 * i + 1)
        return carry

    lax.fori_loop(0, (n_far + 1) // 2, far_step, 0)
    o_s = acc_scr[:, :LANES] / acc_scr[:, LANES:]

    gn = gn_ref[0]
    for h in range(NSA_HEADS):
        r = slice(h * TQ, (h + 1) * TQ)
        o = (gn[:, h:h + 1] * o_c[r] + gn[:, NSA_HEADS + h:NSA_HEADS + h + 1] * o_s[r]
             + gn[:, 2 * NSA_HEADS + h:2 * NSA_HEADS + h + 1] * o_w[r])
        oa_ref[0, :, h * LANES:(h + 1) * LANES] = o.astype(BF16)

    win_b = SWA_WIN_CHUNKS * TQ
    j_b = lax.broadcasted_iota(jnp.int32, (win_b, LANES), 0)
    lane_b = lax.broadcasted_iota(jnp.int32, (win_b, LANES), 1)
    invalid_b = jnp.where((lane_b == 0) & (j_b < (SWA_WIN_CHUNKS - 1 - c) * TQ), 1.0, 0.0).astype(BF16)
    s_b = _dot_nt(_lanes(_stack_heads(qb_ref[0], SWA_HEADS), neg_q),
                  _lanes(_window_rows(kb_ref, s0, SWA_WIN_CHUNKS), invalid_b)) + tb_ref[...]
    sink = sink_ref[...]
    m_b = jnp.maximum(jnp.max(s_b, axis=-1, keepdims=True), sink)
    r_b = _dot(jnp.exp(s_b - m_b).astype(BF16),
               _lanes(_window_rows(vb_ref, s0, SWA_WIN_CHUNKS), ones_v(win_b)))
    o_b = r_b[:, :LANES] / (r_b[:, LANES:] + jnp.exp(sink - m_b))
    for h in range(SWA_HEADS):
        ob_ref[0, :, h * LANES:(h + 1) * LANES] = o_b[h * TQ:(h + 1) * TQ].astype(BF16)


def _mix(qa, qb, gn, kc, vc, kv, ov, ca, tw, ts, tb, sink, *, n_sel):
    b, s, qw = qa.shape
    ncp = kc.shape[1]
    rows = NSA_HEADS * TQ
    once = pl.Buffered(1)
    full = lambda w: pl.BlockSpec(w.shape, lambda i, j: (0,) * w.ndim, pipeline_mode=once)
    per_b = lambda blk, col, w=LANES: pl.BlockSpec((1, blk, w), lambda i, j: (i, 0, col), pipeline_mode=once)
    tile = lambda w: pl.BlockSpec((1, TQ, w), lambda i, j: (i, j, 0))
    return pl.pallas_call(
        functools.partial(_mix_kernel, n_sel=n_sel),
        grid=(b, s // TQ),
        in_specs=[tile(qw), tile(qw), tile(LANES), per_b(ncp, 0), per_b(ncp, 0)]
        + [per_b(s, 0, 2 * LANES), per_b(s, 1, 2 * LANES)] + [per_b(s, col) for col in range(4, 8)]
        + [full(ov), full(ca), full(tw), full(ts), full(tb), full(sink)],
        out_specs=[tile(qw), tile(qw)],
        out_shape=[jax.ShapeDtypeStruct((b, s, qw), BF16)] * 2,
        scratch_shapes=[pltpu.VMEM((rows, 1), F32), pltpu.VMEM((rows, 2 * LANES), F32),
                        pltpu.VMEM((rows, FAR_CHUNK), F32), pltpu.VMEM((rows, FAR_CHUNK), F32)],
        compiler_params=_params("parallel", "arbitrary"),
        name="mix",
    )(qa, qb, gn, kc, vc, kv, kv, kv, kv, kv, kv, ov, ca, tw, ts, tb, sink)


def _merge_kernel(x_ref, g_ref, oa_ref, ob_ref, wga_ref, wgb_ref, wua_ref, wub_ref, wo_ref, o_ref):
    x = x_ref[...]
    h = _rms(x, g_ref[...]).astype(BF16)
    merged = (jax.nn.sigmoid(_dot(h, wga_ref[...])) * _dot(oa_ref[...], wua_ref[...])
              + jax.nn.sigmoid(_dot(h, wgb_ref[...])) * _dot(ob_ref[...], wub_ref[...]))
    o_ref[...] = x + _dot(merged.astype(BF16), wo_ref[...])


def _merge(x2d, g, oa, ob, wga, wgb, wua, wub, wo):
    n, d = x2d.shape
    tm = min(512, n)
    full = lambda w: pl.BlockSpec(w.shape, lambda i: (0, 0))
    row = lambda width: pl.BlockSpec((tm, width), lambda i: (i, 0))
    return pl.pallas_call(
        _merge_kernel,
        grid=(n // tm,),
        in_specs=[row(d), full(g), row(oa.shape[1]), row(ob.shape[1]), full(wga), full(wgb),
                  full(wua), full(wub), full(wo)],
        out_specs=row(d),
        out_shape=jax.ShapeDtypeStruct((n, d), F32),
        compiler_params=_params("parallel"),
        name="merge",
    )(x2d, g, oa, ob, wga, wgb, wua, wub, wo)


def _memkv_kernel(m_ref, g_ref, w_ref, o_ref):
    o_ref[...] = _dot(_rms(m_ref[...], g_ref[...]).astype(BF16), w_ref[...]).astype(BF16)


def _memkv(m2d, g, w):
    n, d = m2d.shape
    tm = min(512, n)
    return pl.pallas_call(
        _memkv_kernel,
        grid=(n // tm,),
        in_specs=[pl.BlockSpec((tm, d), lambda i: (i, 0)), pl.BlockSpec(g.shape, lambda i: (0, 0)),
                  pl.BlockSpec(w.shape, lambda i: (0, 0))],
        out_specs=pl.BlockSpec((tm, w.shape[1]), lambda i: (i, 0)),
        out_shape=jax.ShapeDtypeStruct((n, w.shape[1]), BF16),
        compiler_params=_params("parallel"),
        name="memkv",
    )(m2d, g, w)


def _xattn_kernel(x_ref, g_ref, kv_ref, wq_ref, wo_ref, o_ref):
    x = x_ref[0]
    d = x.shape[1]
    hd = d // XATTN_HEADS
    q = _dot(_rms(x, g_ref[...]).astype(BF16), wq_ref[...]).astype(BF16)
    outs = []
    for h in range(XATTN_HEADS):
        k = kv_ref[0, :, h * hd:(h + 1) * hd]
        v = kv_ref[0, :, d + h * hd:d + (h + 1) * hd]
        s = _dot_nt(q[:, h * hd:(h + 1) * hd], k)
        e = jnp.exp(s - jnp.max(s, axis=-1, keepdims=True))
        outs.append((_dot(e.astype(BF16), v) / jnp.sum(e, axis=-1, keepdims=True)).astype(BF16))
    o_ref[0] = x + _dot(jnp.concatenate(outs, axis=1), wo_ref[...])


def _xattn(x3d, g, kv, wq, wo):
    b, s, d = x3d.shape
    tm = min(512, s)
    full = lambda w: pl.BlockSpec(w.shape, lambda i, j: (0, 0))
    return pl.pallas_call(
        _xattn_kernel,
        grid=(b, s // tm),
        in_specs=[pl.BlockSpec((1, tm, d), lambda i, j: (i, j, 0)), full(g),
                  pl.BlockSpec((1,) + kv.shape[1:], lambda i, j: (i, 0, 0)), full(wq), full(wo)],
        out_specs=pl.BlockSpec((1, tm, d), lambda i, j: (i, j, 0)),
        out_shape=jax.ShapeDtypeStruct((b, s, d), F32),
        compiler_params=_params("parallel", "parallel"),
        name="xattn",
    )(x3d, g, kv, wq, wo)


def _rel_bucket(dist):
    dist = jnp.maximum(dist, 0)
    max_exact = REL_BUCKETS // 2
    d = jnp.maximum(dist, 1).astype(F32)
    large = max_exact + (jnp.log(d / max_exact) / math.log(REL_MAX_DIST / max_exact)
                         * (REL_BUCKETS - max_exact)).astype(jnp.int32)
    large = jnp.minimum(large, REL_BUCKETS - 1)
    return jnp.where(dist < max_exact, dist, large)


def _bias_of_dist(bias, dist):
    bucket = _rel_bucket(dist)[None]
    out = jnp.zeros((bias.shape[1],) + dist.shape, F32)
    for k in range(REL_BUCKETS):
        out = jnp.where(bucket == k, bias[k].reshape((-1,) + (1,) * dist.ndim), out)
    return out


def _expand_heads(w, heads, kv_heads):
    d = w.shape[0]
    group = np.arange(heads) // (heads // kv_heads)
    onehot = jnp.asarray(np.eye(kv_heads, dtype=np.float32)[group])
    w4 = w.reshape(d, heads, 1, HEAD_DIM) * onehot[None, :, :, None]
    return w4.reshape(d, heads * kv_heads * HEAD_DIM)


def _expand_heads_rows(w, heads, kv_heads):
    return _expand_heads(w.T, heads, kv_heads).T


def _bias_tile(bias, shift, window_chunks, hi):
    win = window_chunks * TQ
    dist = jnp.arange(TQ)[:, None] + (window_chunks - 1) * TQ - jnp.arange(win)[None, :]
    vals = _bias_of_dist(bias, dist) - shift[:, None, None]
    return jnp.where((dist >= 0) & (dist < hi), vals, NEG).reshape(-1, win)


def _cmp_tables(w1, pe):
    half = CMP_BLOCK // 2
    eye = jnp.eye(NSA_GROUPS, dtype=F32)
    w = w1.reshape(2, half, HEAD_DIM, CMP_HIDDEN)
    wexp = jnp.einsum('aldh,gk->algdkh', w, eye)
    wexp = wexp.reshape(2, half * NSA_GROUPS * HEAD_DIM, NSA_GROUPS * CMP_HIDDEN)
    pexp = jnp.broadcast_to(pe.reshape(2, half, 1, HEAD_DIM), (2, half, NSA_GROUPS, HEAD_DIM))
    return wexp[0].astype(BF16), wexp[1].astype(BF16), pexp.reshape(2, -1)


def _block_diag2(w):
    z = jnp.zeros_like(w)
    return jnp.concatenate([jnp.concatenate([w, z], axis=1), jnp.concatenate([z, w], axis=1)], axis=0)


def kernel(x, mem, norm_ffn1, w1_gate, w1_up, w1_down, norm_mix, w_in, cmp_pe_k, cmp_w1_k, cmp_w2_k,
           cmp_pe_v, cmp_w1_v, cmp_w2_v, attn_sinks, rel_bias, w_up_a, w_up_b, w_out, norm_xattn,
           norm_mem, w_xq, w_xkv, w_xo, norm_ffn2, w2_gate, w2_up, w2_down, norm_final):
    b, s, d = x.shape
    n = b * s
    assert norm_ffn1.shape[0] == 1, "single-layer kernel"
    assert s % FAR_CHUNK == 0 and s >= NSA_WINDOW + TQ
    n_cmp = (s - CMP_BLOCK) // CMP_STRIDE + 1
    ncp = s // CMP_STRIDE
    n_sel = s // SEL_BLOCK
    assert n_sel <= LANES, "selection blocks must fit the 128 mask lanes"
    bf = lambda w: w.astype(BF16)
    scale = HEAD_DIM ** -0.5

    wi = w_in[0]
    nq = NSA_HEADS * HEAD_DIM
    nkv = NSA_GROUPS * HEAD_DIM
    o_g = nq + 6 * nkv
    o_qb = o_g + 3 * NSA_HEADS
    o_kb = o_qb + SWA_HEADS * HEAD_DIM
    o_ga = o_kb + 2 * SWA_KV_HEADS * HEAD_DIM
    wqa = bf(_expand_heads(wi[:, :nq] * scale, NSA_HEADS, NSA_GROUPS))
    wqb = bf(_expand_heads(wi[:, o_qb:o_kb] * scale, SWA_HEADS, SWA_KV_HEADS))
    wkv = bf(jnp.concatenate([wi[:, nq + 2 * nkv:o_g], wi[:, o_kb:o_ga]], axis=1))
    wc = bf(wi[:, nq:nq + 2 * nkv])
    wgn = bf(jnp.pad(wi[:, o_g:o_qb], ((0, 0), (0, LANES - 3 * NSA_HEADS))))
    wga = bf(wi[:, o_ga:o_ga + d])
    wgb = bf(wi[:, o_ga + d:o_ga + 2 * d])
    wua = bf(_expand_heads_rows(w_up_a[0], NSA_HEADS, NSA_GROUPS))
    wub = bf(_expand_heads_rows(w_up_b[0], SWA_HEADS, SWA_KV_HEADS))
    wka, wkb, pek = _cmp_tables(cmp_w1_k[0], cmp_pe_k[0])
    wva, wvb, pev = _cmp_tables(cmp_w1_v[0], cmp_pe_v[0])
    w2k = bf(_block_diag2(cmp_w2_k[0]))
    w2v = bf(_block_diag2(cmp_w2_v[0]))

    bias_a = rel_bias[:, :NSA_HEADS]
    bias_b = rel_bias[:, NSA_HEADS:]
    far_bias = _bias_of_dist(bias_a, jnp.full((1,), REL_MAX_DIST, jnp.int32))[:, 0]
    no_shift = jnp.zeros((NSA_HEADS,), F32)
    tw = _bias_tile(bias_a, no_shift, NSA_WIN_CHUNKS, NSA_WINDOW)
    ts = _bias_tile(bias_a, far_bias, NSA_WIN_CHUNKS, NSA_WIN_CHUNKS * TQ)
    tb = _bias_tile(bias_b, no_shift, SWA_WIN_CHUNKS, SWA_WINDOW)
    dist_c = (jnp.arange(TQ)[:, None] + (CMP_WIN_BACK * CMP_STRIDE - (CMP_BLOCK - 1))
              - CMP_STRIDE * jnp.arange(CMP_BLOCK)[None, :])
    dcv = jnp.where(dist_c < REL_MAX_DIST, _bias_of_dist(bias_a, dist_c) - far_bias[:, None, None], 0.0)
    dcv = jnp.where(dist_c >= 0, dcv, NEG).reshape(NSA_HEADS * TQ, CMP_BLOCK)
    dc_hi = dcv.astype(BF16)
    dc_lo = (dcv - dc_hi.astype(F32)).astype(BF16)
    after = jnp.full((NSA_HEADS * TQ, 1), NEG, BF16)
    ca = jnp.concatenate([dc_hi, dc_lo, after,
                          jnp.zeros((NSA_HEADS * TQ, LANES - 2 * CMP_BLOCK - 1), BF16)], axis=1)
    sink = jnp.repeat(attn_sinks[0], TQ)[:, None]

    cs = np.arange(ncp)[:, None] * CMP_STRIDE
    ss = np.arange(LANES)[None, :] * SEL_BLOCK
    ovl = np.maximum(np.minimum(cs + CMP_BLOCK, ss + SEL_BLOCK) - np.maximum(cs, ss), 0) / CMP_BLOCK
    ovl[n_cmp:] = 0.0
    ov = jnp.asarray(ovl, dtype=BF16)

    row = lambda g: g.reshape(1, -1)
    x1 = _ffn(x.reshape(n, d), row(norm_ffn1[0]), bf(w1_gate[0]), bf(w1_up[0]), bf(w1_down[0]),
              row(norm_final), final_norm=False)
    qa, qb, kv, kcf, vcf, gn = _inproj(x1, row(norm_mix[0]), wqa, wqb, wkv, wc, wgn, seq_len=s)
    tok_w = CMP_STRIDE * LANES
    kc, vc = _compress(kcf.reshape(b, ncp, tok_w), vcf.reshape(b, ncp, tok_w), pek, pev,
                       wka, wkb, wva, wvb, w2k, w2v)
    oa, ob = _mix(qa.reshape(b, s, -1), qb.reshape(b, s, -1), gn.reshape(b, s, LANES), kc, vc,
                  kv.reshape(b, s, -1), ov, ca, tw, ts, tb, sink, n_sel=n_sel)
    x2 = _merge(x1, row(norm_mix[0]), oa.reshape(n, -1), ob.reshape(n, -1), wga, wgb, wua, wub,
                bf(w_out[0]))
    mkv = _memkv(mem.reshape(-1, d), row(norm_mem[0]), bf(w_xkv[0]))
    x3 = _xattn(x2.reshape(b, s, d), row(norm_xattn[0]), mkv.reshape(b, -1, 2 * d),
                bf(w_xq[0] * (d // XATTN_HEADS) ** -0.5), bf(w_xo[0]))
    out = _ffn(x3.reshape(n, d), row(norm_ffn2[0]), bf(w2_gate[0]), bf(w2_up[0]), bf(w2_down[0]),
               row(norm_final), final_norm=True)
    return out.reshape(b, s, d)
```

```python
import functools
import math

import numpy as np
import jax
import jax.numpy as jnp
from jax import lax
from jax.experimental import pallas as pl
from jax.experimental.pallas import tpu as pltpu

F32 = jnp.float32
BF16 = jnp.bfloat16

HEAD_DIM = 64
NSA_HEADS = 8
NSA_GROUPS = 2
HEADS_PER_GROUP = NSA_HEADS // NSA_GROUPS
CMP_BLOCK = 32
CMP_STRIDE = 16
CMP_HIDDEN = 256
SEL_BLOCK = 64
SEL_TOPN = 16
NSA_WINDOW = 512
SWA_HEADS = 8
SWA_KV_HEADS = 2
SWA_WINDOW = 128
REL_BUCKETS = 32
REL_MAX_DIST = 128
XATTN_HEADS = 4
EPS = 1e-6
NEG = -1e30

LANES = 128
TQ = 128
FAR_CHUNK = 512
NSA_WIN_CHUNKS = NSA_WINDOW // TQ + 1
SWA_WIN_CHUNKS = SWA_WINDOW // TQ + 1
CMP_WIN_BACK = 16
VMEM_LIMIT = 56 * 1024 * 1024


def _rms(x, g):
    return x * lax.rsqrt(jnp.mean(x * x, axis=-1, keepdims=True) + EPS) * g


def _dot(a, b):
    return jnp.dot(a, b, preferred_element_type=F32)


def _dot_nt(a, b):
    return lax.dot_general(a, b, (((1,), (1,)), ((), ())), preferred_element_type=F32)


def _params(*sem):
    return pltpu.CompilerParams(dimension_semantics=sem, vmem_limit_bytes=VMEM_LIMIT)


def _ffn_kernel(x_ref, g_ref, wg_ref, wu_ref, wd_ref, gf_ref, o_ref, h_scr, acc_scr, *, final_norm):
    j = pl.program_id(1)

    @pl.when(j == 0)
    def _():
        h_scr[...] = _rms(x_ref[...], g_ref[...]).astype(BF16)
        acc_scr[...] = jnp.zeros_like(acc_scr)

    h = h_scr[...]
    a = _dot(h, wg_ref[...])
    u = _dot(h, wu_ref[...])
    act = a * jax.nn.sigmoid(a) * u
    acc_scr[...] += _dot(act.astype(BF16), wd_ref[...])

    @pl.when(j == pl.num_programs(1) - 1)
    def _():
        y = x_ref[...] + 0.5 * acc_scr[...]
        if final_norm:
            y = _rms(y, gf_ref[...])
        o_ref[...] = y


def _ffn(x2d, g, wg, wu, wd, gf, *, final_norm):
    n, d = x2d.shape
    dff = wg.shape[1]
    tm = min(1024, n)
    tf = 256
    return pl.pallas_call(
        functools.partial(_ffn_kernel, final_norm=final_norm),
        grid=(n // tm, dff // tf),
        in_specs=[
            pl.BlockSpec((tm, d), lambda i, j: (i, 0)),
            pl.BlockSpec((1, d), lambda i, j: (0, 0)),
            pl.BlockSpec((d, tf), lambda i, j: (0, j)),
            pl.BlockSpec((d, tf), lambda i, j: (0, j)),
            pl.BlockSpec((tf, d), lambda i, j: (j, 0)),
            pl.BlockSpec((1, d), lambda i, j: (0, 0)),
        ],
        out_specs=pl.BlockSpec((tm, d), lambda i, j: (i, 0)),
        out_shape=jax.ShapeDtypeStruct((n, d), F32),
        scratch_shapes=[pltpu.VMEM((tm, d), BF16), pltpu.VMEM((tm, d), F32)],
        compiler_params=_params("parallel", "arbitrary"),
        name="ffn",
    )(x2d, g, wg, wu, wd, gf)


def _expand_head_pairs(r, heads, kv_heads):
    lane = lax.broadcasted_iota(jnp.int32, (r.shape[0], LANES), 1)
    low = lane < HEAD_DIM
    tiles = []
    for p in range(heads // 2):
        pair = r[:, p * LANES:(p + 1) * LANES]
        swapped = pltpu.roll(pair, HEAD_DIM, 1)
        if (2 * p) // (heads // kv_heads) == 0:
            tiles += [jnp.where(low, pair, 0.0), jnp.where(low, swapped, 0.0)]
        else:
            tiles += [jnp.where(low, 0.0, swapped), jnp.where(low, 0.0, pair)]
    return tiles


def _compact_head_pairs(tiles, heads, kv_heads):
    lane = lax.broadcasted_iota(jnp.int32, tiles[0].shape, 1)
    low = lane < HEAD_DIM
    out = []
    for p in range(heads // 2):
        a, b = tiles[2 * p], tiles[2 * p + 1]
        if (2 * p) // (heads // kv_heads) == 0:
            out.append(jnp.where(low, a, pltpu.roll(b, HEAD_DIM, 1)))
        else:
            out.append(jnp.where(low, pltpu.roll(a, HEAD_DIM, 1), b))
    return jnp.concatenate(out, axis=1)


def _inproj_kernel(x_ref, g_ref, wqa_ref, wqb_ref, wkv_ref, wc_ref, wg_ref,
                   qa_ref, qb_ref, kv_ref, kc_ref, vc_ref, gn_ref, *, seq_len):
    tm = x_ref.shape[0]
    h = _rms(x_ref[...], g_ref[...]).astype(BF16)
    for w_ref, q_ref, heads, kv_heads in ((wqa_ref, qa_ref, NSA_HEADS, NSA_GROUPS),
                                          (wqb_ref, qb_ref, SWA_HEADS, SWA_KV_HEADS)):
        tiles = _expand_head_pairs(_dot(h, w_ref[...]), heads, kv_heads)
        for hh, t in enumerate(tiles):
            q_ref[:, hh * LANES:(hh + 1) * LANES] = t.astype(BF16)
    kv = _dot(h, wkv_ref[...]).astype(BF16)
    pos = (pl.program_id(0) * tm + lax.broadcasted_iota(jnp.int32, (tm, LANES), 0)) % seq_len
    lane = lax.broadcasted_iota(jnp.int32, (tm, LANES), 1)
    block_onehot = jnp.where(lane == pos // SEL_BLOCK, 1.0, 0.0).astype(BF16)
    ones = jnp.ones((tm, LANES), BF16)
    for i in range(kv.shape[1] // LANES):
        kv_ref[:, 2 * i * LANES:(2 * i + 1) * LANES] = kv[:, i * LANES:(i + 1) * LANES]
        kv_ref[:, (2 * i + 1) * LANES:(2 * i + 2) * LANES] = block_onehot if i % 2 == 0 else ones
    c = _dot(h, wc_ref[...])
    kc_ref[...] = c[:, :LANES]
    vc_ref[...] = c[:, LANES:]
    gn_ref[...] = jax.nn.sigmoid(_dot(h, wg_ref[...]))


def _inproj(x2d, g, wqa, wqb, wkv, wc, wg, *, seq_len):
    n, d = x2d.shape
    kv_w = 2 * wkv.shape[1]
    tm = min(512, n)
    full = lambda w: pl.BlockSpec(w.shape, lambda i: (0, 0))
    row = lambda width: pl.BlockSpec((tm, width), lambda i: (i, 0))
    return pl.pallas_call(
        functools.partial(_inproj_kernel, seq_len=seq_len),
        grid=(n // tm,),
        in_specs=[row(d), full(g), full(wqa), full(wqb), full(wkv), full(wc), full(wg)],
        out_specs=[row(2 * wqa.shape[1]), row(2 * wqb.shape[1]), row(kv_w), row(LANES), row(LANES),
                   row(LANES)],
        out_shape=[
            jax.ShapeDtypeStruct((n, 2 * wqa.shape[1]), BF16),
            jax.ShapeDtypeStruct((n, 2 * wqb.shape[1]), BF16),
            jax.ShapeDtypeStruct((n, kv_w), BF16),
            jax.ShapeDtypeStruct((n, LANES), F32),
            jax.ShapeDtypeStruct((n, LANES), F32),
            jax.ShapeDtypeStruct((n, LANES), F32),
        ],
        compiler_params=_params("parallel"),
        name="inproj",
    )(x2d, g, wqa, wqb, wkv, wc, wg)


def _compress_kernel(ak_ref, av_ref, pek_ref, pev_ref, wka_ref, wkb_ref, wva_ref, wvb_ref,
                     w2k_ref, w2v_ref, kc_ref, vc_ref):
    def one(a_ref, pe_ref, wa_ref, wb_ref, w2_ref, o_ref):
        a = a_ref[0]
        rows = a.shape[0]
        u = _dot((a + pe_ref[0:1, :]).astype(BF16), wa_ref[...])
        v = _dot((a + pe_ref[1:2, :]).astype(BF16), wb_ref[...])
        hid = u + pltpu.roll(v, rows - 1, 0)
        hid = jax.nn.gelu(hid, approximate=True)
        o_ref[0] = _dot(hid.astype(BF16), w2_ref[...]).astype(BF16)

    one(ak_ref, pek_ref, wka_ref, wkb_ref, w2k_ref, kc_ref)
    one(av_ref, pev_ref, wva_ref, wvb_ref, w2v_ref, vc_ref)


def _compress(ak, av, pek, pev, wka, wkb, wva, wvb, w2k, w2v):
    b, ncp, width = ak.shape
    full = lambda w: pl.BlockSpec(w.shape, lambda i: (0,) * w.ndim)
    per_b = lambda w: pl.BlockSpec((1, ncp, w), lambda i: (i, 0, 0))
    return pl.pallas_call(
        _compress_kernel,
        grid=(b,),
        in_specs=[per_b(width), per_b(width), full(pek), full(pev), full(wka), full(wkb),
                  full(wva), full(wvb), full(w2k), full(w2v)],
        out_specs=[per_b(LANES), per_b(LANES)],
        out_shape=[jax.ShapeDtypeStruct((b, ncp, LANES), BF16)] * 2,
        compiler_params=_params("parallel"),
        name="compress",
    )(ak, av, pek, pev, wka, wkb, wva, wvb, w2k, w2v)


def _stack_heads(q, heads):
    return jnp.concatenate([q[:, h * LANES:(h + 1) * LANES] for h in range(heads)], axis=0)


def _window_rows(ref, s0, window_chunks, fallback):
    parts = []
    for r in range(window_chunks):
        start = s0 + (r - (window_chunks - 1)) * TQ
        start = jnp.where(start < 0, fallback, start)
        parts.append(ref[0, pl.ds(pl.multiple_of(start, TQ), TQ), :])
    return jnp.concatenate(parts, axis=0)


def _lanes(a, b):
    return jnp.concatenate([a, b], axis=1)


def _per_head(tiles_per_group):
    return jnp.concatenate([tiles_per_group[h // HEADS_PER_GROUP] for h in range(NSA_HEADS)], axis=0)


def _mix_kernel(qa_ref, qb_ref, gn_ref, kc_ref, vc_ref, ks_ref, vs_ref, kw_ref, vw_ref, kb_ref, vb_ref,
                ov_ref, ca_ref, tw0_ref, tw1_ref, ts_ref, tb_ref, sink_ref,
                oa_ref, ob_ref, m_scr, acc_scr, sfa_scr, sfb_scr, *, n_sel):
    c = pl.program_id(1)
    s0 = c * TQ
    rows = NSA_HEADS * TQ
    ncp = kc_ref.shape[1]
    nsp = ov_ref.shape[1]
    win = NSA_WIN_CHUNKS * TQ

    q = _stack_heads(qa_ref[0], NSA_HEADS)
    t_col = s0 + lax.broadcasted_iota(jnp.int32, (rows, 1), 0) % TQ

    n_i = lax.broadcasted_iota(jnp.int32, (ncp, LANES), 0)
    lane_c = lax.broadcasted_iota(jnp.int32, (ncp, LANES), 1)
    w0 = s0 // CMP_STRIDE - CMP_WIN_BACK
    slot = jnp.where(lane_c < CMP_BLOCK, lane_c, lane_c - CMP_BLOCK)
    place = ((lane_c < 2 * CMP_BLOCK) & (n_i == w0 + slot)) | (
        (lane_c == 2 * CMP_BLOCK) & (n_i >= w0 + CMP_BLOCK))
    kc_aug = _lanes(kc_ref[0], jnp.where(place, 1.0, 0.0).astype(BF16))
    s_c = _dot_nt(_lanes(q, ca_ref[...]), kc_aug)
    e_c = jnp.exp2(s_c - jnp.max(s_c, axis=-1, keepdims=True))
    l_c = jnp.sum(e_c, axis=-1, keepdims=True)
    p_c = e_c * jnp.where(t_col >= CMP_BLOCK - 1, 1.0 / l_c, 0.0)
    o_c = _dot(p_c.astype(BF16), vc_ref[0])

    t_q = s0 + lax.broadcasted_iota(jnp.int32, (TQ, 1), 0)
    cur_q = t_q // SEL_BLOCK
    blk_l = lax.broadcasted_iota(jnp.int32, (TQ, nsp), 1)
    forced = (blk_l == 0) | (blk_l == cur_q) | (blk_l == cur_q - 1)
    causal_blk = blk_l * SEL_BLOCK <= t_q
    cand = causal_blk & jnp.logical_not(forced)
    n_sweeps = min(SEL_TOPN, n_sel) - 3
    blk_s = lax.broadcasted_iota(jnp.int32, (nsp, TQ), 0).astype(F32)
    first_blk = (s0 - NSA_WINDOW) // SEL_BLOCK

    near_q, far_q = [], []
    for g in range(NSA_GROUPS):
        base = g * HEADS_PER_GROUP * TQ
        psum = p_c[base:base + TQ]
        for h in range(1, HEADS_PER_GROUP):
            psum = psum + p_c[base + h * TQ:base + (h + 1) * TQ]
        hi = psum.astype(BF16)
        lo = (psum - hi.astype(F32)).astype(BF16)
        imp = _dot(hi, ov_ref[...]) + _dot(lo, ov_ref[...])
        v_t = jnp.where(cand, imp, -1.0).T
        for _ in range(n_sweeps):
            mx = jnp.max(v_t, axis=0, keepdims=True)
            first = jnp.min(jnp.where(v_t == mx, blk_s, float(nsp)), axis=0, keepdims=True)
            v_t = jnp.where(blk_s == first, -2.0, v_t)
        chosen = (forced | (v_t.T == -2.0)) & causal_blk
        near_q.append(jnp.where(chosen, 0.0, NEG).astype(BF16))
        far_q.append(jnp.where(chosen & (blk_l < first_blk), 0.0, NEG).astype(BF16))

    s_len = ks_ref.shape[1]
    spare = s_len - TQ
    causal_q = _per_head([jnp.where(causal_blk, 0.0, NEG).astype(BF16)] * NSA_GROUPS)

    def add_tiles(s, first_ref, last_ref):
        first = s[:, :TQ] if first_ref is None else s[:, :TQ] + first_ref[...]
        return jnp.concatenate([first, s[:, TQ:win - 2 * TQ], s[:, win - 2 * TQ:] + last_ref[...]], axis=1)

    s_w = _dot_nt(_lanes(q, causal_q), _window_rows(kw_ref, s0, NSA_WIN_CHUNKS, spare))
    s_w = add_tiles(s_w, tw0_ref, tw1_ref)
    e_w = jnp.exp2((s_w - jnp.max(s_w, axis=-1, keepdims=True)).astype(BF16))
    r_w = _dot(e_w, _window_rows(vw_ref, s0, NSA_WIN_CHUNKS, spare))
    o_w = r_w[:, :LANES] / r_w[:, LANES:]

    s_n = _dot_nt(_lanes(q, _per_head(near_q)), _window_rows(ks_ref, s0, NSA_WIN_CHUNKS, spare))
    s_n = add_tiles(s_n, None, ts_ref)
    m_n = jnp.max(s_n, axis=-1, keepdims=True)
    m_scr[...] = m_n
    acc_scr[...] = _dot(jnp.exp2((s_n - m_n).astype(BF16)),
                        _window_rows(vs_ref, s0, NSA_WIN_CHUNKS, spare))

    q_far = _lanes(q, _per_head(far_q))
    n_far = jnp.maximum(s0 - NSA_WINDOW + FAR_CHUNK - 1, 0) // FAR_CHUNK
    last_chunk = s_len // FAR_CHUNK - 1

    def far_rows(j):
        return pl.ds(pl.multiple_of(jnp.minimum(j, last_chunk) * FAR_CHUNK, FAR_CHUNK), FAR_CHUNK)

    def far_logits(j):
        return _dot_nt(q_far, ks_ref[0, far_rows(j), :])

    def far_update(s_f, j):
        m_old = m_scr[...]
        m_new = jnp.maximum(m_old, jnp.max(s_f, axis=-1, keepdims=True))
        e_f = jnp.exp2((s_f - m_new).astype(BF16))
        acc_scr[...] = jnp.exp2(m_old - m_new) * acc_scr[...] + _dot(e_f, vs_ref[0, far_rows(j), :])
        m_scr[...] = m_new

    sfa_scr[...] = far_logits(0)

    def far_step(i, carry):
        sfb_scr[...] = far_logits(2 * i + 1)
        far_update(sfa_scr[...], 2 * i)
        sfa_scr[...] = far_logits(2 * i + 2)
        far_update(sfb_scr[...], 2 * i + 1)
        return carry

    lax.fori_loop(0, (n_far + 1) // 2, far_step, 0)
    o_s = acc_scr[:, :LANES] / acc_scr[:, LANES:]

    gn = gn_ref[0]
    tiles = []
    for h in range(NSA_HEADS):
        r = slice(h * TQ, (h + 1) * TQ)
        tiles.append(gn[:, h:h + 1] * o_c[r] + gn[:, NSA_HEADS + h:NSA_HEADS + h + 1] * o_s[r]
                     + gn[:, 2 * NSA_HEADS + h:2 * NSA_HEADS + h + 1] * o_w[r])
    oa_ref[0] = _compact_head_pairs(tiles, NSA_HEADS, NSA_GROUPS).astype(BF16)

    s_b = _dot_nt(_lanes(_stack_heads(qb_ref[0], SWA_HEADS), causal_q),
                  _window_rows(kb_ref, s0, SWA_WIN_CHUNKS, spare)) + tb_ref[...]
    sink = sink_ref[...]
    m_b = jnp.maximum(jnp.max(s_b, axis=-1, keepdims=True), sink)
    r_b = _dot(jnp.exp2((s_b - m_b).astype(BF16)), _window_rows(vb_ref, s0, SWA_WIN_CHUNKS, spare))
    o_b = r_b[:, :LANES] / (r_b[:, LANES:] + jnp.exp2(sink - m_b))
    ob_ref[0] = _compact_head_pairs([o_b[h * TQ:(h + 1) * TQ] for h in range(SWA_HEADS)],
                                    SWA_HEADS, SWA_KV_HEADS).astype(BF16)


def _mix(qa, qb, gn, kc, vc, kv, ov, ca, tw0, tw1, ts, tb, sink, *, n_sel):
    b, s, qw = qa.shape
    ncp = kc.shape[1]
    rows = NSA_HEADS * TQ
    once = pl.Buffered(1)
    full = lambda w: pl.BlockSpec(w.shape, lambda i, j: (0,) * w.ndim, pipeline_mode=once)
    per_b = lambda blk, col, w: pl.BlockSpec((1, blk, w), lambda i, j: (i, 0, col), pipeline_mode=once)
    tile = lambda w: pl.BlockSpec((1, TQ, w), lambda i, j: (i, j, 0))
    return pl.pallas_call(
        functools.partial(_mix_kernel, n_sel=n_sel),
        grid=(b, s // TQ),
        in_specs=[tile(qw), tile(qw), tile(LANES), per_b(ncp, 0, LANES), per_b(ncp, 0, LANES)]
        + [per_b(s, col, 2 * LANES) for col in range(6)]
        + [full(ov), full(ca), full(tw0), full(tw1), full(ts), full(tb), full(sink)],
        out_specs=[tile(qw // 2), tile(qw // 2)],
        out_shape=[jax.ShapeDtypeStruct((b, s, qw // 2), BF16)] * 2,
        scratch_shapes=[pltpu.VMEM((rows, 1), F32), pltpu.VMEM((rows, 2 * LANES), F32),
                        pltpu.VMEM((rows, FAR_CHUNK), F32), pltpu.VMEM((rows, FAR_CHUNK), F32)],
        compiler_params=_params("parallel", "arbitrary"),
        name="mix",
    )(qa, qb, gn, kc, vc, kv, kv, kv, kv, kv, kv, ov, ca, tw0, tw1, ts, tb, sink)


def _merge_kernel(x_ref, g_ref, oa_ref, ob_ref, wga_ref, wgb_ref, wua_ref, wub_ref, wo_ref, o_ref):
    x = x_ref[...]
    h = _rms(x, g_ref[...]).astype(BF16)
    merged = (jax.nn.sigmoid(_dot(h, wga_ref[...])) * _dot(oa_ref[...], wua_ref[...])
              + jax.nn.sigmoid(_dot(h, wgb_ref[...])) * _dot(ob_ref[...], wub_ref[...]))
    o_ref[...] = x + _dot(merged.astype(BF16), wo_ref[...])


def _merge(x2d, g, oa, ob, wga, wgb, wua, wub, wo):
    n, d = x2d.shape
    tm = min(512, n)
    full = lambda w: pl.BlockSpec(w.shape, lambda i: (0, 0))
    row = lambda width: pl.BlockSpec((tm, width), lambda i: (i, 0))
    return pl.pallas_call(
        _merge_kernel,
        grid=(n // tm,),
        in_specs=[row(d), full(g), row(oa.shape[1]), row(ob.shape[1]), full(wga), full(wgb),
                  full(wua), full(wub), full(wo)],
        out_specs=row(d),
        out_shape=jax.ShapeDtypeStruct((n, d), F32),
        compiler_params=_params("parallel"),
        name="merge",
    )(x2d, g, oa, ob, wga, wgb, wua, wub, wo)


def _memkv_kernel(m_ref, g_ref, w_ref, o_ref):
    o_ref[...] = _dot(_rms(m_ref[...], g_ref[...]).astype(BF16), w_ref[...]).astype(BF16)


def _memkv(m2d, g, w):
    n, d = m2d.shape
    tm = min(512, n)
    return pl.pallas_call(
        _memkv_kernel,
        grid=(n // tm,),
        in_specs=[pl.BlockSpec((tm, d), lambda i: (i, 0)), pl.BlockSpec(g.shape, lambda i: (0, 0)),
                  pl.BlockSpec(w.shape, lambda i: (0, 0))],
        out_specs=pl.BlockSpec((tm, w.shape[1]), lambda i: (i, 0)),
        out_shape=jax.ShapeDtypeStruct((n, w.shape[1]), BF16),
        compiler_params=_params("parallel"),
        name="memkv",
    )(m2d, g, w)


def _xattn_kernel(x_ref, g_ref, kv_ref, wq_ref, wo_ref, o_ref):
    x = x_ref[0]
    d = x.shape[1]
    hd = d // XATTN_HEADS
    q = _dot(_rms(x, g_ref[...]).astype(BF16), wq_ref[...]).astype(BF16)
    outs = []
    for h in range(XATTN_HEADS):
        k = kv_ref[0, :, h * hd:(h + 1) * hd]
        v = kv_ref[0, :, d + h * hd:d + (h + 1) * hd]
        s = _dot_nt(q[:, h * hd:(h + 1) * hd], k)
        e = jnp.exp2(s - jnp.max(s, axis=-1, keepdims=True))
        outs.append((_dot(e.astype(BF16), v) / jnp.sum(e, axis=-1, keepdims=True)).astype(BF16))
    o_ref[0] = x + _dot(jnp.concatenate(outs, axis=1), wo_ref[...])


def _xattn(x3d, g, kv, wq, wo):
    b, s, d = x3d.shape
    tm = min(512, s)
    full = lambda w: pl.BlockSpec(w.shape, lambda i, j: (0, 0))
    return pl.pallas_call(
        _xattn_kernel,
        grid=(b, s // tm),
        in_specs=[pl.BlockSpec((1, tm, d), lambda i, j: (i, j, 0)), full(g),
                  pl.BlockSpec((1,) + kv.shape[1:], lambda i, j: (i, 0, 0)), full(wq), full(wo)],
        out_specs=pl.BlockSpec((1, tm, d), lambda i, j: (i, j, 0)),
        out_shape=jax.ShapeDtypeStruct((b, s, d), F32),
        compiler_params=_params("parallel", "parallel"),
        name="xattn",
    )(x3d, g, kv, wq, wo)


def _rel_bucket(dist):
    dist = jnp.maximum(dist, 0)
    max_exact = REL_BUCKETS // 2
    d = jnp.maximum(dist, 1).astype(F32)
    large = max_exact + (jnp.log(d / max_exact) / math.log(REL_MAX_DIST / max_exact)
                         * (REL_BUCKETS - max_exact)).astype(jnp.int32)
    large = jnp.minimum(large, REL_BUCKETS - 1)
    return jnp.where(dist < max_exact, dist, large)


def _bias_of_dist(bias, dist):
    bucket = _rel_bucket(dist)[None]
    out = jnp.zeros((bias.shape[1],) + dist.shape, F32)
    for k in range(REL_BUCKETS):
        out = jnp.where(bucket == k, bias[k].reshape((-1,) + (1,) * dist.ndim), out)
    return out


def _bias_tile(bias, shift, window_chunks, hi):
    win = window_chunks * TQ
    dist = jnp.arange(TQ)[:, None] + (window_chunks - 1) * TQ - jnp.arange(win)[None, :]
    vals = _bias_of_dist(bias, dist) - shift[:, None, None]
    return jnp.where((dist >= 0) & (dist < hi), vals, NEG).reshape(-1, win)


def _cmp_tables(w1, pe):
    half = CMP_BLOCK // 2
    eye = jnp.eye(NSA_GROUPS, dtype=F32)
    w = w1.reshape(2, half, HEAD_DIM, CMP_HIDDEN)
    wexp = jnp.einsum('aldh,gk->algdkh', w, eye)
    wexp = wexp.reshape(2, half * NSA_GROUPS * HEAD_DIM, NSA_GROUPS * CMP_HIDDEN)
    pexp = jnp.broadcast_to(pe.reshape(2, half, 1, HEAD_DIM), (2, half, NSA_GROUPS, HEAD_DIM))
    return wexp[0].astype(BF16), wexp[1].astype(BF16), pexp.reshape(2, -1)


def _block_diag2(w):
    z = jnp.zeros_like(w)
    return jnp.concatenate([jnp.concatenate([w, z], axis=1), jnp.concatenate([z, w], axis=1)], axis=0)


def kernel(x, mem, norm_ffn1, w1_gate, w1_up, w1_down, norm_mix, w_in, cmp_pe_k, cmp_w1_k, cmp_w2_k,
           cmp_pe_v, cmp_w1_v, cmp_w2_v, attn_sinks, rel_bias, w_up_a, w_up_b, w_out, norm_xattn,
           norm_mem, w_xq, w_xkv, w_xo, norm_ffn2, w2_gate, w2_up, w2_down, norm_final):
    b, s, d = x.shape
    n = b * s
    assert norm_ffn1.shape[0] == 1, "single-layer kernel"
    assert s % FAR_CHUNK == 0 and s >= 2 * NSA_WINDOW
    n_cmp = (s - CMP_BLOCK) // CMP_STRIDE + 1
    ncp = s // CMP_STRIDE
    n_sel = s // SEL_BLOCK
    assert n_sel <= LANES, "selection blocks must fit the 128 mask lanes"
    bf = lambda w: w.astype(BF16)
    scale = HEAD_DIM ** -0.5
    log2e = math.log2(math.e)

    wi = w_in[0]
    nq = NSA_HEADS * HEAD_DIM
    nkv = NSA_GROUPS * HEAD_DIM
    o_g = nq + 6 * nkv
    o_qb = o_g + 3 * NSA_HEADS
    o_kb = o_qb + SWA_HEADS * HEAD_DIM
    o_ga = o_kb + 2 * SWA_KV_HEADS * HEAD_DIM
    wqa = bf(wi[:, :nq] * (scale * log2e))
    wqb = bf(wi[:, o_qb:o_kb] * (scale * log2e))
    wkv = bf(jnp.concatenate([wi[:, nq + 2 * nkv:o_g], wi[:, o_kb:o_ga]], axis=1))
    wc = bf(wi[:, nq:nq + 2 * nkv])
    wgn = bf(jnp.pad(wi[:, o_g:o_qb], ((0, 0), (0, LANES - 3 * NSA_HEADS))))
    wga = bf(wi[:, o_ga:o_ga + d])
    wgb = bf(wi[:, o_ga + d:o_ga + 2 * d])
    wka, wkb, pek = _cmp_tables(cmp_w1_k[0], cmp_pe_k[0])
    wva, wvb, pev = _cmp_tables(cmp_w1_v[0], cmp_pe_v[0])
    w2k = bf(_block_diag2(cmp_w2_k[0]))
    w2v = bf(_block_diag2(cmp_w2_v[0]))

    bias_a = rel_bias[:, :NSA_HEADS] * log2e
    bias_b = rel_bias[:, NSA_HEADS:] * log2e
    far_bias = _bias_of_dist(bias_a, jnp.full((1,), REL_MAX_DIST, jnp.int32))[:, 0]
    no_shift = jnp.zeros((SWA_HEADS,), F32)
    tw = _bias_tile(bias_a, far_bias, NSA_WIN_CHUNKS, NSA_WINDOW)
    ts = _bias_tile(bias_a, far_bias, NSA_WIN_CHUNKS, NSA_WIN_CHUNKS * TQ)
    tb = _bias_tile(bias_b, no_shift, SWA_WIN_CHUNKS, SWA_WINDOW)
    dist_c = (jnp.arange(TQ)[:, None] + (CMP_WIN_BACK * CMP_STRIDE - (CMP_BLOCK - 1))
              - CMP_STRIDE * jnp.arange(CMP_BLOCK)[None, :])
    dcv = jnp.where(dist_c < REL_MAX_DIST, _bias_of_dist(bias_a, dist_c) - far_bias[:, None, None], 0.0)
    dcv = jnp.where(dist_c >= 0, dcv, NEG).reshape(NSA_HEADS * TQ, CMP_BLOCK)
    dc_hi = dcv.astype(BF16)
    dc_lo = (dcv - dc_hi.astype(F32)).astype(BF16)
    after = jnp.full((NSA_HEADS * TQ, 1), NEG, BF16)
    ca = jnp.concatenate([dc_hi, dc_lo, after,
                          jnp.zeros((NSA_HEADS * TQ, LANES - 2 * CMP_BLOCK - 1), BF16)], axis=1)
    sink = jnp.repeat(attn_sinks[0] * log2e, TQ)[:, None]

    cs = np.arange(ncp)[:, None] * CMP_STRIDE
    ss = np.arange(LANES)[None, :] * SEL_BLOCK
    ovl = np.maximum(np.minimum(cs + CMP_BLOCK, ss + SEL_BLOCK) - np.maximum(cs, ss), 0) / CMP_BLOCK
    ovl[n_cmp:] = 0.0
    ov = jnp.asarray(ovl, dtype=BF16)

    row = lambda g: g.reshape(1, -1)
    x1 = _ffn(x.reshape(n, d), row(norm_ffn1[0]), bf(w1_gate[0]), bf(w1_up[0]), bf(w1_down[0]),
              row(norm_final), final_norm=False)
    qa, qb, kv, kcf, vcf, gn = _inproj(x1, row(norm_mix[0]), wqa, wqb, wkv, wc, wgn, seq_len=s)
    tok_w = CMP_STRIDE * LANES
    kc, vc = _compress(kcf.reshape(b, ncp, tok_w), vcf.reshape(b, ncp, tok_w), pek, pev,
                       wka, wkb, wva, wvb, w2k, w2v)
    oa, ob = _mix(qa.reshape(b, s, -1), qb.reshape(b, s, -1), gn.reshape(b, s, LANES), kc, vc,
                  kv.reshape(b, s, -1), ov, ca, tw[:, :TQ], tw[:, -2 * TQ:], ts[:, -2 * TQ:], tb, sink,
                  n_sel=n_sel)
    x2 = _merge(x1, row(norm_mix[0]), oa.reshape(n, -1), ob.reshape(n, -1), wga, wgb,
                bf(w_up_a[0]), bf(w_up_b[0]), bf(w_out[0]))
    mkv = _memkv(mem.reshape(-1, d), row(norm_mem[0]), bf(w_xkv[0]))
    x3 = _xattn(x2.reshape(b, s, d), row(norm_xattn[0]), mkv.reshape(b, -1, 2 * d),
                bf(w_xq[0] * ((d // XATTN_HEADS) ** -0.5 * log2e)), bf(w_xo[0]))
    out = _ffn(x3.reshape(n, d), row(norm_ffn2[0]), bf(w2_gate[0]), bf(w2_up[0]), bf(w2_down[0]),
               row(norm_final), final_norm=True)
    return out.reshape(b, s, d)
```

```python
import functools
import math

import numpy as np
import jax
import jax.numpy as jnp
from jax import lax
from jax.experimental import pallas as pl
from jax.experimental.pallas import tpu as pltpu

F32 = jnp.float32
BF16 = jnp.bfloat16

HEAD_DIM = 64
NSA_HEADS = 8
NSA_GROUPS = 2
HEADS_PER_GROUP = NSA_HEADS // NSA_GROUPS
CMP_BLOCK = 32
CMP_STRIDE = 16
CMP_HIDDEN = 256
SEL_BLOCK = 64
SEL_TOPN = 16
NSA_WINDOW = 512
SWA_HEADS = 8
SWA_KV_HEADS = 2
SWA_WINDOW = 128
REL_BUCKETS = 32
REL_MAX_DIST = 128
XATTN_HEADS = 4
EPS = 1e-6
NEG = -1e30

LANES = 128
TQ = 128
FAR_CHUNK = 512
NSA_WIN_CHUNKS = NSA_WINDOW // TQ + 1
SWA_WIN_CHUNKS = SWA_WINDOW // TQ + 1
CMP_WIN_BACK = 16
VMEM_LIMIT = 56 * 1024 * 1024


def _rms(x, g):
    return x * lax.rsqrt(jnp.mean(x * x, axis=-1, keepdims=True) + EPS) * g


def _dot(a, b):
    return jnp.dot(a, b, preferred_element_type=F32)


def _dot_nt(a, b):
    return lax.dot_general(a, b, (((1,), (1,)), ((), ())), preferred_element_type=F32)


def _params(*sem):
    return pltpu.CompilerParams(dimension_semantics=sem, vmem_limit_bytes=VMEM_LIMIT)


def _ffn_kernel(x_ref, g_ref, wg_ref, wu_ref, wd_ref, gf_ref, o_ref, h_scr, acc_scr, *, final_norm):
    j = pl.program_id(1)

    @pl.when(j == 0)
    def _():
        h_scr[...] = _rms(x_ref[...], g_ref[...]).astype(BF16)
        acc_scr[...] = jnp.zeros_like(acc_scr)

    h = h_scr[...]
    a = _dot(h, wg_ref[...])
    u = _dot(h, wu_ref[...])
    act = a * jax.nn.sigmoid(a) * u
    acc_scr[...] += _dot(act.astype(BF16), wd_ref[...])

    @pl.when(j == pl.num_programs(1) - 1)
    def _():
        y = x_ref[...] + 0.5 * acc_scr[...]
        if final_norm:
            y = _rms(y, gf_ref[...])
        o_ref[...] = y


def _ffn(x2d, g, wg, wu, wd, gf, *, final_norm):
    n, d = x2d.shape
    dff = wg.shape[1]
    tm = min(1024, n)
    tf = 256
    return pl.pallas_call(
        functools.partial(_ffn_kernel, final_norm=final_norm),
        grid=(n // tm, dff // tf),
        in_specs=[
            pl.BlockSpec((tm, d), lambda i, j: (i, 0)),
            pl.BlockSpec((1, d), lambda i, j: (0, 0)),
            pl.BlockSpec((d, tf), lambda i, j: (0, j)),
            pl.BlockSpec((d, tf), lambda i, j: (0, j)),
            pl.BlockSpec((tf, d), lambda i, j: (j, 0)),
            pl.BlockSpec((1, d), lambda i, j: (0, 0)),
        ],
        out_specs=pl.BlockSpec((tm, d), lambda i, j: (i, 0)),
        out_shape=jax.ShapeDtypeStruct((n, d), F32),
        scratch_shapes=[pltpu.VMEM((tm, d), BF16), pltpu.VMEM((tm, d), F32)],
        compiler_params=_params("parallel", "arbitrary"),
        name="ffn",
    )(x2d, g, wg, wu, wd, gf)


def _expand_head_pairs(r, heads, kv_heads):
    lane = lax.broadcasted_iota(jnp.int32, (r.shape[0], LANES), 1)
    low = lane < HEAD_DIM
    tiles = []
    for p in range(heads // 2):
        pair = r[:, p * LANES:(p + 1) * LANES]
        swapped = pltpu.roll(pair, HEAD_DIM, 1)
        if (2 * p) // (heads // kv_heads) == 0:
            tiles += [jnp.where(low, pair, 0.0), jnp.where(low, swapped, 0.0)]
        else:
            tiles += [jnp.where(low, 0.0, swapped), jnp.where(low, 0.0, pair)]
    return tiles


def _compact_head_pairs(tiles, heads, kv_heads):
    lane = lax.broadcasted_iota(jnp.int32, tiles[0].shape, 1)
    low = lane < HEAD_DIM
    out = []
    for p in range(heads // 2):
        a, b = tiles[2 * p], tiles[2 * p + 1]
        if (2 * p) // (heads // kv_heads) == 0:
            out.append(jnp.where(low, a, pltpu.roll(b, HEAD_DIM, 1)))
        else:
            out.append(jnp.where(low, pltpu.roll(a, HEAD_DIM, 1), b))
    return jnp.concatenate(out, axis=1)


def _inproj_kernel(x_ref, g_ref, wqa_ref, wqb_ref, wkv_ref, wc_ref, wg_ref,
                   qa_ref, qb_ref, kv_ref, kc_ref, vc_ref, gn_ref, *, seq_len):
    tm = x_ref.shape[0]
    h = _rms(x_ref[...], g_ref[...]).astype(BF16)
    for w_ref, q_ref, heads, kv_heads in ((wqa_ref, qa_ref, NSA_HEADS, NSA_GROUPS),
                                          (wqb_ref, qb_ref, SWA_HEADS, SWA_KV_HEADS)):
        tiles = _expand_head_pairs(_dot(h, w_ref[...]), heads, kv_heads)
        for hh, t in enumerate(tiles):
            q_ref[:, hh * LANES:(hh + 1) * LANES] = t.astype(BF16)
    kv = _dot(h, wkv_ref[...]).astype(BF16)
    pos = (pl.program_id(0) * tm + lax.broadcasted_iota(jnp.int32, (tm, LANES), 0)) % seq_len
    lane = lax.broadcasted_iota(jnp.int32, (tm, LANES), 1)
    block_onehot = jnp.where(lane == pos // SEL_BLOCK, 1.0, 0.0).astype(BF16)
    ones = jnp.ones((tm, LANES), BF16)
    for i in range(kv.shape[1] // LANES):
        kv_ref[:, 2 * i * LANES:(2 * i + 1) * LANES] = kv[:, i * LANES:(i + 1) * LANES]
        kv_ref[:, (2 * i + 1) * LANES:(2 * i + 2) * LANES] = block_onehot if i % 2 == 0 else ones
    c = _dot(h, wc_ref[...])
    kc_ref[...] = c[:, :LANES]
    vc_ref[...] = c[:, LANES:]
    gn_ref[...] = jax.nn.sigmoid(_dot(h, wg_ref[...]))


def _inproj(x2d, g, wqa, wqb, wkv, wc, wg, *, seq_len):
    n, d = x2d.shape
    kv_w = 2 * wkv.shape[1]
    tm = min(512, n)
    full = lambda w: pl.BlockSpec(w.shape, lambda i: (0, 0))
    row = lambda width: pl.BlockSpec((tm, width), lambda i: (i, 0))
    return pl.pallas_call(
        functools.partial(_inproj_kernel, seq_len=seq_len),
        grid=(n // tm,),
        in_specs=[row(d), full(g), full(wqa), full(wqb), full(wkv), full(wc), full(wg)],
        out_specs=[row(2 * wqa.shape[1]), row(2 * wqb.shape[1]), row(kv_w), row(LANES), row(LANES),
                   row(LANES)],
        out_shape=[
            jax.ShapeDtypeStruct((n, 2 * wqa.shape[1]), BF16),
            jax.ShapeDtypeStruct((n, 2 * wqb.shape[1]), BF16),
            jax.ShapeDtypeStruct((n, kv_w), BF16),
            jax.ShapeDtypeStruct((n, LANES), F32),
            jax.ShapeDtypeStruct((n, LANES), F32),
            jax.ShapeDtypeStruct((n, LANES), F32),
        ],
        compiler_params=_params("parallel"),
        name="inproj",
    )(x2d, g, wqa, wqb, wkv, wc, wg)


def _compress_kernel(ak_ref, av_ref, pek_ref, pev_ref, wka_ref, wkb_ref, wva_ref, wvb_ref,
                     w2k_ref, w2v_ref, ov_ref, kc_ref, vc_ref):
    def one(a_ref, pe_ref, wa_ref, wb_ref, w2_ref):
        a = a_ref[0]
        rows = a.shape[0]
        u = _dot((a + pe_ref[0:1, :]).astype(BF16), wa_ref[...])
        v = _dot((a + pe_ref[1:2, :]).astype(BF16), wb_ref[...])
        hid = u + pltpu.roll(v, rows - 1, 0)
        hid = jax.nn.gelu(hid, approximate=True)
        return _dot(hid.astype(BF16), w2_ref[...]).astype(BF16)

    kc_ref[0] = one(ak_ref, pek_ref, wka_ref, wkb_ref, w2k_ref)
    vc_ref[0, :, :LANES] = one(av_ref, pev_ref, wva_ref, wvb_ref, w2v_ref)
    vc_ref[0, :, LANES:2 * LANES] = ov_ref[...]
    vc_ref[0, :, 2 * LANES:] = jnp.ones(ov_ref.shape, BF16)


def _compress(ak, av, pek, pev, wka, wkb, wva, wvb, w2k, w2v, ov):
    b, ncp, width = ak.shape
    full = lambda w: pl.BlockSpec(w.shape, lambda i: (0,) * w.ndim)
    per_b = lambda w: pl.BlockSpec((1, ncp, w), lambda i: (i, 0, 0))
    return pl.pallas_call(
        _compress_kernel,
        grid=(b,),
        in_specs=[per_b(width), per_b(width), full(pek), full(pev), full(wka), full(wkb),
                  full(wva), full(wvb), full(w2k), full(w2v), full(ov)],
        out_specs=[per_b(LANES), per_b(3 * LANES)],
        out_shape=[jax.ShapeDtypeStruct((b, ncp, LANES), BF16),
                   jax.ShapeDtypeStruct((b, ncp, 3 * LANES), BF16)],
        compiler_params=_params("parallel"),
        name="compress",
    )(ak, av, pek, pev, wka, wkb, wva, wvb, w2k, w2v, ov)


def _stack_heads(q, heads):
    return jnp.concatenate([q[:, h * LANES:(h + 1) * LANES] for h in range(heads)], axis=0)


def _window_rows(ref, s0, window_chunks, fallback):
    parts = []
    for r in range(window_chunks):
        start = s0 + (r - (window_chunks - 1)) * TQ
        start = jnp.where(start < 0, fallback, start)
        parts.append(ref[0, pl.ds(pl.multiple_of(start, TQ), TQ), :])
    return jnp.concatenate(parts, axis=0)


def _lanes(a, b):
    return jnp.concatenate([a, b], axis=1)


def _per_head(tiles_per_group):
    return jnp.concatenate([tiles_per_group[h // HEADS_PER_GROUP] for h in range(NSA_HEADS)], axis=0)


def _mix_kernel(qa_ref, qb_ref, gn_ref, kc_ref, vc_ref, ks_ref, vs_ref, kw_ref, vw_ref, kb_ref, vb_ref,
                ca_ref, tw0_ref, tw1_ref, ts_ref, tb_ref, sink_ref,
                oa_ref, ob_ref, m_scr, acc_scr, sfa_scr, sfb_scr, *, n_sel):
    c = pl.program_id(1)
    s0 = c * TQ
    rows = NSA_HEADS * TQ
    ncp = kc_ref.shape[1]
    nsp = LANES
    win = NSA_WIN_CHUNKS * TQ

    q = _stack_heads(qa_ref[0], NSA_HEADS)
    t_col = s0 + lax.broadcasted_iota(jnp.int32, (rows, 1), 0) % TQ

    n_i = lax.broadcasted_iota(jnp.int32, (ncp, LANES), 0)
    lane_c = lax.broadcasted_iota(jnp.int32, (ncp, LANES), 1)
    w0 = s0 // CMP_STRIDE - CMP_WIN_BACK
    slot = jnp.where(lane_c < CMP_BLOCK, lane_c, lane_c - CMP_BLOCK)
    place = ((lane_c < 2 * CMP_BLOCK) & (n_i == w0 + slot)) | (
        (lane_c == 2 * CMP_BLOCK) & (n_i >= w0 + CMP_BLOCK))
    kc_aug = _lanes(kc_ref[0], jnp.where(place, 1.0, 0.0).astype(BF16))
    s_c = _dot_nt(_lanes(q, ca_ref[...]), kc_aug)
    e_c = jnp.exp2(s_c - jnp.max(s_c, axis=-1, keepdims=True)).astype(BF16)
    r_c = _dot(e_c, vc_ref[0])
    inv_c = jnp.where(t_col >= CMP_BLOCK - 1, 1.0 / r_c[:, 2 * LANES:], 0.0)
    o_c = r_c[:, :LANES] * inv_c
    imp_h = r_c[:, LANES:2 * LANES] * inv_c

    t_q = s0 + lax.broadcasted_iota(jnp.int32, (TQ, 1), 0)
    cur_q = t_q // SEL_BLOCK
    blk_l = lax.broadcasted_iota(jnp.int32, (TQ, nsp), 1)
    forced = (blk_l == 0) | (blk_l == cur_q) | (blk_l == cur_q - 1)
    causal_blk = blk_l * SEL_BLOCK <= t_q
    cand = causal_blk & jnp.logical_not(forced)
    n_sweeps = min(SEL_TOPN, n_sel) - 3
    blk_s = lax.broadcasted_iota(jnp.int32, (nsp, TQ), 0).astype(F32)
    first_blk = (s0 - NSA_WINDOW) // SEL_BLOCK

    near_q, far_q = [], []
    for g in range(NSA_GROUPS):
        base = g * HEADS_PER_GROUP * TQ
        imp = imp_h[base:base + TQ]
        for h in range(1, HEADS_PER_GROUP):
            imp = imp + imp_h[base + h * TQ:base + (h + 1) * TQ]
        v_t = jnp.where(cand, imp, -1.0).T
        for _ in range(n_sweeps):
            mx = jnp.max(v_t, axis=0, keepdims=True)
            first = jnp.min(jnp.where(v_t == mx, blk_s, float(nsp)), axis=0, keepdims=True)
            v_t = jnp.where(blk_s == first, -2.0, v_t)
        chosen = (forced | (v_t.T == -2.0)) & causal_blk
        near_q.append(jnp.where(chosen, 0.0, NEG).astype(BF16))
        far_q.append(jnp.where(chosen & (blk_l < first_blk), 0.0, NEG).astype(BF16))

    s_len = ks_ref.shape[1]
    spare = s_len - TQ
    causal_q = _per_head([jnp.where(causal_blk, 0.0, NEG).astype(BF16)] * NSA_GROUPS)

    def add_tiles(s, first_ref, last_ref):
        first = s[:, :TQ] if first_ref is None else s[:, :TQ] + first_ref[...]
        return jnp.concatenate([first, s[:, TQ:win - 2 * TQ], s[:, win - 2 * TQ:] + last_ref[...]], axis=1)

    s_w = _dot_nt(_lanes(q, causal_q), _window_rows(kw_ref, s0, NSA_WIN_CHUNKS, spare))
    s_w = add_tiles(s_w, tw0_ref, tw1_ref)
    e_w = jnp.exp2((s_w - jnp.max(s_w, axis=-1, keepdims=True)).astype(BF16))
    r_w = _dot(e_w, _window_rows(vw_ref, s0, NSA_WIN_CHUNKS, spare))
    o_w = r_w[:, :LANES] / r_w[:, LANES:]

    s_n = _dot_nt(_lanes(q, _per_head(near_q)), _window_rows(ks_ref, s0, NSA_WIN_CHUNKS, spare))
    s_n = add_tiles(s_n, None, ts_ref)
    m_n = jnp.max(s_n, axis=-1, keepdims=True)
    m_scr[...] = m_n
    acc_scr[...] = _dot(jnp.exp2((s_n - m_n).astype(BF16)),
                        _window_rows(vs_ref, s0, NSA_WIN_CHUNKS, spare))

    q_far = _lanes(q, _per_head(far_q))
    n_far = jnp.maximum(s0 - NSA_WINDOW + FAR_CHUNK - 1, 0) // FAR_CHUNK
    last_chunk = s_len // FAR_CHUNK - 1
    group_rows = [slice(g * rows // NSA_GROUPS, (g + 1) * rows // NSA_GROUPS) for g in range(NSA_GROUPS)]

    def far_rows(j):
        return pl.ds(pl.multiple_of(jnp.minimum(j, last_chunk) * FAR_CHUNK, FAR_CHUNK), FAR_CHUNK)

    def far_logits(j, buf):
        for r in group_rows:
            buf[r, :] = _dot_nt(q_far[r], ks_ref[0, far_rows(j), :])

    def far_update(buf, j):
        for r in group_rows:
            s_f = buf[r, :]
            m_old = m_scr[r, :]
            m_new = jnp.maximum(m_old, jnp.max(s_f, axis=-1, keepdims=True))
            e_f = jnp.exp2((s_f - m_new).astype(BF16))
            acc_scr[r, :] = jnp.exp2(m_old - m_new) * acc_scr[r, :] + _dot(e_f, vs_ref[0, far_rows(j), :])
            m_scr[r, :] = m_new

    far_logits(0, sfa_scr)

    def far_step(i, carry):
        far_logits(2 * i + 1, sfb_scr)
        far_update(sfa_scr, 2 * i)
        far_logits(2 * i + 2, sfa_scr)
        far_update(sfb_scr, 2 * i + 1)
        return carry

    lax.fori_loop(0, (n_far + 1) // 2, far_step, 0)
    o_s = acc_scr[:, :LANES] / acc_scr[:, LANES:]

    gn = gn_ref[0]
    tiles = []
    for h in range(NSA_HEADS):
        r = slice(h * TQ, (h + 1) * TQ)
        tiles.append(gn[:, h:h + 1] * o_c[r] + gn[:, NSA_HEADS + h:NSA_HEADS + h + 1] * o_s[r]
                     + gn[:, 2 * NSA_HEADS + h:2 * NSA_HEADS + h + 1] * o_w[r])
    oa_ref[0] = _compact_head_pairs(tiles, NSA_HEADS, NSA_GROUPS).astype(BF16)

    s_b = _dot_nt(_lanes(_stack_heads(qb_ref[0], SWA_HEADS), causal_q),
                  _window_rows(kb_ref, s0, SWA_WIN_CHUNKS, spare)) + tb_ref[...]
    sink = sink_ref[...]
    m_b = jnp.maximum(jnp.max(s_b, axis=-1, keepdims=True), sink)
    r_b = _dot(jnp.exp2((s_b - m_b).astype(BF16)), _window_rows(vb_ref, s0, SWA_WIN_CHUNKS, spare))
    o_b = r_b[:, :LANES] / (r_b[:, LANES:] + jnp.exp2(sink - m_b))
    ob_ref[0] = _compact_head_pairs([o_b[h * TQ:(h + 1) * TQ] for h in range(SWA_HEADS)],
                                    SWA_HEADS, SWA_KV_HEADS).astype(BF16)


def _mix(qa, qb, gn, kc, vc, kv, ca, tw0, tw1, ts, tb, sink, *, n_sel):
    b, s, qw = qa.shape
    ncp = kc.shape[1]
    rows = NSA_HEADS * TQ
    once = pl.Buffered(1)
    full = lambda w: pl.BlockSpec(w.shape, lambda i, j: (0,) * w.ndim, pipeline_mode=once)
    per_b = lambda blk, col, w: pl.BlockSpec((1, blk, w), lambda i, j: (i, 0, col), pipeline_mode=once)
    tile = lambda w: pl.BlockSpec((1, TQ, w), lambda i, j: (i, j, 0))
    return pl.pallas_call(
        functools.partial(_mix_kernel, n_sel=n_sel),
        grid=(b, s // TQ),
        in_specs=[tile(qw), tile(qw), tile(LANES), per_b(ncp, 0, LANES), per_b(ncp, 0, 3 * LANES)]
        + [per_b(s, col, 2 * LANES) for col in range(6)]
        + [full(ca), full(tw0), full(tw1), full(ts), full(tb), full(sink)],
        out_specs=[tile(qw // 2), tile(qw // 2)],
        out_shape=[jax.ShapeDtypeStruct((b, s, qw // 2), BF16)] * 2,
        scratch_shapes=[pltpu.VMEM((rows, 1), F32), pltpu.VMEM((rows, 2 * LANES), F32),
                        pltpu.VMEM((rows, FAR_CHUNK), F32), pltpu.VMEM((rows, FAR_CHUNK), F32)],
        compiler_params=_params("parallel", "arbitrary"),
        name="mix",
    )(qa, qb, gn, kc, vc, kv, kv, kv, kv, kv, kv, ca, tw0, tw1, ts, tb, sink)


def _merge_kernel(x_ref, g_ref, oa_ref, ob_ref, wga_ref, wgb_ref, wua_ref, wub_ref, wo_ref, o_ref):
    x = x_ref[...]
    h = _rms(x, g_ref[...]).astype(BF16)
    merged = (jax.nn.sigmoid(_dot(h, wga_ref[...])) * _dot(oa_ref[...], wua_ref[...])
              + jax.nn.sigmoid(_dot(h, wgb_ref[...])) * _dot(ob_ref[...], wub_ref[...]))
    o_ref[...] = x + _dot(merged.astype(BF16), wo_ref[...])


def _merge(x2d, g, oa, ob, wga, wgb, wua, wub, wo):
    n, d = x2d.shape
    tm = min(512, n)
    full = lambda w: pl.BlockSpec(w.shape, lambda i: (0, 0))
    row = lambda width: pl.BlockSpec((tm, width), lambda i: (i, 0))
    return pl.pallas_call(
        _merge_kernel,
        grid=(n // tm,),
        in_specs=[row(d), full(g), row(oa.shape[1]), row(ob.shape[1]), full(wga), full(wgb),
                  full(wua), full(wub), full(wo)],
        out_specs=row(d),
        out_shape=jax.ShapeDtypeStruct((n, d), F32),
        compiler_params=_params("parallel"),
        name="merge",
    )(x2d, g, oa, ob, wga, wgb, wua, wub, wo)


def _memkv_kernel(m_ref, g_ref, w_ref, o_ref):
    o_ref[...] = _dot(_rms(m_ref[...], g_ref[...]).astype(BF16), w_ref[...]).astype(BF16)


def _memkv(m2d, g, w):
    n, d = m2d.shape
    tm = min(512, n)
    return pl.pallas_call(
        _memkv_kernel,
        grid=(n // tm,),
        in_specs=[pl.BlockSpec((tm, d), lambda i: (i, 0)), pl.BlockSpec(g.shape, lambda i: (0, 0)),
                  pl.BlockSpec(w.shape, lambda i: (0, 0))],
        out_specs=pl.BlockSpec((tm, w.shape[1]), lambda i: (i, 0)),
        out_shape=jax.ShapeDtypeStruct((n, w.shape[1]), BF16),
        compiler_params=_params("parallel"),
        name="memkv",
    )(m2d, g, w)


def _xattn_kernel(x_ref, g_ref, kv_ref, wq_ref, wo_ref, o_ref):
    x = x_ref[0]
    d = x.shape[1]
    hd = d // XATTN_HEADS
    q = _dot(_rms(x, g_ref[...]).astype(BF16), wq_ref[...]).astype(BF16)
    outs = []
    for h in range(XATTN_HEADS):
        k = kv_ref[0, :, h * hd:(h + 1) * hd]
        v = kv_ref[0, :, d + h * hd:d + (h + 1) * hd]
        s = _dot_nt(q[:, h * hd:(h + 1) * hd], k)
        e = jnp.exp2(s - jnp.max(s, axis=-1, keepdims=True))
        outs.append((_dot(e.astype(BF16), v) / jnp.sum(e, axis=-1, keepdims=True)).astype(BF16))
    o_ref[0] = x + _dot(jnp.concatenate(outs, axis=1), wo_ref[...])


def _xattn(x3d, g, kv, wq, wo):
    b, s, d = x3d.shape
    tm = min(512, s)
    full = lambda w: pl.BlockSpec(w.shape, lambda i, j: (0, 0))
    return pl.pallas_call(
        _xattn_kernel,
        grid=(b, s // tm),
        in_specs=[pl.BlockSpec((1, tm, d), lambda i, j: (i, j, 0)), full(g),
                  pl.BlockSpec((1,) + kv.shape[1:], lambda i, j: (i, 0, 0)), full(wq), full(wo)],
        out_specs=pl.BlockSpec((1, tm, d), lambda i, j: (i, j, 0)),
        out_shape=jax.ShapeDtypeStruct((b, s, d), F32),
        compiler_params=_params("parallel", "parallel"),
        name="xattn",
    )(x3d, g, kv, wq, wo)


def _rel_bucket(dist):
    dist = jnp.maximum(dist, 0)
    max_exact = REL_BUCKETS // 2
    d = jnp.maximum(dist, 1).astype(F32)
    large = max_exact + (jnp.log(d / max_exact) / math.log(REL_MAX_DIST / max_exact)
                         * (REL_BUCKETS - max_exact)).astype(jnp.int32)
    large = jnp.minimum(large, REL_BUCKETS - 1)
    return jnp.where(dist < max_exact, dist, large)


def _bias_of_dist(bias, dist):
    bucket = _rel_bucket(dist)[None]
    out = jnp.zeros((bias.shape[1],) + dist.shape, F32)
    for k in range(REL_BUCKETS):
        out = jnp.where(bucket == k, bias[k].reshape((-1,) + (1,) * dist.ndim), out)
    return out


def _bias_tile(bias, shift, window_chunks, hi):
    win = window_chunks * TQ
    dist = jnp.arange(TQ)[:, None] + (window_chunks - 1) * TQ - jnp.arange(win)[None, :]
    vals = _bias_of_dist(bias, dist) - shift[:, None, None]
    return jnp.where((dist >= 0) & (dist < hi), vals, NEG).reshape(-1, win)


def _cmp_tables(w1, pe):
    half = CMP_BLOCK // 2
    eye = jnp.eye(NSA_GROUPS, dtype=F32)
    w = w1.reshape(2, half, HEAD_DIM, CMP_HIDDEN)
    wexp = jnp.einsum('aldh,gk->algdkh', w, eye)
    wexp = wexp.reshape(2, half * NSA_GROUPS * HEAD_DIM, NSA_GROUPS * CMP_HIDDEN)
    pexp = jnp.broadcast_to(pe.reshape(2, half, 1, HEAD_DIM), (2, half, NSA_GROUPS, HEAD_DIM))
    return wexp[0].astype(BF16), wexp[1].astype(BF16), pexp.reshape(2, -1)


def _block_diag2(w):
    z = jnp.zeros_like(w)
    return jnp.concatenate([jnp.concatenate([w, z], axis=1), jnp.concatenate([z, w], axis=1)], axis=0)


def kernel(x, mem, norm_ffn1, w1_gate, w1_up, w1_down, norm_mix, w_in, cmp_pe_k, cmp_w1_k, cmp_w2_k,
           cmp_pe_v, cmp_w1_v, cmp_w2_v, attn_sinks, rel_bias, w_up_a, w_up_b, w_out, norm_xattn,
           norm_mem, w_xq, w_xkv, w_xo, norm_ffn2, w2_gate, w2_up, w2_down, norm_final):
    b, s, d = x.shape
    n = b * s
    assert norm_ffn1.shape[0] == 1, "single-layer kernel"
    assert s % FAR_CHUNK == 0 and s >= 2 * NSA_WINDOW
    n_cmp = (s - CMP_BLOCK) // CMP_STRIDE + 1
    ncp = s // CMP_STRIDE
    n_sel = s // SEL_BLOCK
    assert n_sel <= LANES, "selection blocks must fit the 128 mask lanes"
    bf = lambda w: w.astype(BF16)
    scale = HEAD_DIM ** -0.5
    log2e = math.log2(math.e)

    wi = w_in[0]
    nq = NSA_HEADS * HEAD_DIM
    nkv = NSA_GROUPS * HEAD_DIM
    o_g = nq + 6 * nkv
    o_qb = o_g + 3 * NSA_HEADS
    o_kb = o_qb + SWA_HEADS * HEAD_DIM
    o_ga = o_kb + 2 * SWA_KV_HEADS * HEAD_DIM
    wqa = bf(wi[:, :nq] * (scale * log2e))
    wqb = bf(wi[:, o_qb:o_kb] * (scale * log2e))
    wkv = bf(jnp.concatenate([wi[:, nq + 2 * nkv:o_g], wi[:, o_kb:o_ga]], axis=1))
    wc = bf(wi[:, nq:nq + 2 * nkv])
    wgn = bf(jnp.pad(wi[:, o_g:o_qb], ((0, 0), (0, LANES - 3 * NSA_HEADS))))
    wga = bf(wi[:, o_ga:o_ga + d])
    wgb = bf(wi[:, o_ga + d:o_ga + 2 * d])
    wka, wkb, pek = _cmp_tables(cmp_w1_k[0], cmp_pe_k[0])
    wva, wvb, pev = _cmp_tables(cmp_w1_v[0], cmp_pe_v[0])
    w2k = bf(_block_diag2(cmp_w2_k[0]))
    w2v = bf(_block_diag2(cmp_w2_v[0]))

    bias_a = rel_bias[:, :NSA_HEADS] * log2e
    bias_b = rel_bias[:, NSA_HEADS:] * log2e
    far_bias = _bias_of_dist(bias_a, jnp.full((1,), REL_MAX_DIST, jnp.int32))[:, 0]
    no_shift = jnp.zeros((SWA_HEADS,), F32)
    tw = _bias_tile(bias_a, far_bias, NSA_WIN_CHUNKS, NSA_WINDOW)
    ts = _bias_tile(bias_a, far_bias, NSA_WIN_CHUNKS, NSA_WIN_CHUNKS * TQ)
    tb = _bias_tile(bias_b, no_shift, SWA_WIN_CHUNKS, SWA_WINDOW)
    dist_c = (jnp.arange(TQ)[:, None] + (CMP_WIN_BACK * CMP_STRIDE - (CMP_BLOCK - 1))
              - CMP_STRIDE * jnp.arange(CMP_BLOCK)[None, :])
    dcv = jnp.where(dist_c < REL_MAX_DIST, _bias_of_dist(bias_a, dist_c) - far_bias[:, None, None], 0.0)
    dcv = jnp.where(dist_c >= 0, dcv, NEG).reshape(NSA_HEADS * TQ, CMP_BLOCK)
    dc_hi = dcv.astype(BF16)
    dc_lo = (dcv - dc_hi.astype(F32)).astype(BF16)
    after = jnp.full((NSA_HEADS * TQ, 1), NEG, BF16)
    ca = jnp.concatenate([dc_hi, dc_lo, after,
                          jnp.zeros((NSA_HEADS * TQ, LANES - 2 * CMP_BLOCK - 1), BF16)], axis=1)
    sink = jnp.repeat(attn_sinks[0] * log2e, TQ)[:, None]

    cs = np.arange(ncp)[:, None] * CMP_STRIDE
    ss = np.arange(LANES)[None, :] * SEL_BLOCK
    ovl = np.maximum(np.minimum(cs + CMP_BLOCK, ss + SEL_BLOCK) - np.maximum(cs, ss), 0) / CMP_BLOCK
    ovl[n_cmp:] = 0.0
    ov = jnp.asarray(ovl, dtype=BF16)

    row = lambda g: g.reshape(1, -1)
    x1 = _ffn(x.reshape(n, d), row(norm_ffn1[0]), bf(w1_gate[0]), bf(w1_up[0]), bf(w1_down[0]),
              row(norm_final), final_norm=False)
    qa, qb, kv, kcf, vcf, gn = _inproj(x1, row(norm_mix[0]), wqa, wqb, wkv, wc, wgn, seq_len=s)
    tok_w = CMP_STRIDE * LANES
    kc, vc = _compress(kcf.reshape(b, ncp, tok_w), vcf.reshape(b, ncp, tok_w), pek, pev,
                       wka, wkb, wva, wvb, w2k, w2v, ov)
    oa, ob = _mix(qa.reshape(b, s, -1), qb.reshape(b, s, -1), gn.reshape(b, s, LANES), kc, vc,
                  kv.reshape(b, s, -1), ca, tw[:, :TQ], tw[:, -2 * TQ:], ts[:, -2 * TQ:], tb, sink,
                  n_sel=n_sel)
    x2 = _merge(x1, row(norm_mix[0]), oa.reshape(n, -1), ob.reshape(n, -1), wga, wgb,
                bf(w_up_a[0]), bf(w_up_b[0]), bf(w_out[0]))
    mkv = _memkv(mem.reshape(-1, d), row(norm_mem[0]), bf(w_xkv[0]))
    x3 = _xattn(x2.reshape(b, s, d), row(norm_xattn[0]), mkv.reshape(b, -1, 2 * d),
                bf(w_xq[0] * ((d // XATTN_HEADS) ** -0.5 * log2e)), bf(w_xo[0]))
    out = _ffn(x3.reshape(n, d), row(norm_ffn2[0]), bf(w2_gate[0]), bf(w2_up[0]), bf(w2_down[0]),
               row(norm_final), final_norm=True)
    return out.reshape(b, s, d)
```

```python
import functools
import math

import numpy as np
import jax
import jax.numpy as jnp
from jax import lax
from jax.experimental import pallas as pl
from jax.experimental.pallas import tpu as pltpu

F32 = jnp.float32
BF16 = jnp.bfloat16

HEAD_DIM = 64
NSA_HEADS = 8
NSA_GROUPS = 2
HEADS_PER_GROUP = NSA_HEADS // NSA_GROUPS
CMP_BLOCK = 32
CMP_STRIDE = 16
CMP_HIDDEN = 256
SEL_BLOCK = 64
SEL_TOPN = 16
NSA_WINDOW = 512
SWA_HEADS = 8
SWA_KV_HEADS = 2
SWA_WINDOW = 128
REL_BUCKETS = 32
REL_MAX_DIST = 128
XATTN_HEADS = 4
EPS = 1e-6
NEG = -1e30

LANES = 128
TQ = 128
FAR_CHUNK = 512
NSA_WIN_CHUNKS = NSA_WINDOW // TQ + 1
SWA_WIN_CHUNKS = SWA_WINDOW // TQ + 1
CMP_WIN_BACK = 16
VMEM_LIMIT = 56 * 1024 * 1024


def _rms(x, g):
    return x * lax.rsqrt(jnp.mean(x * x, axis=-1, keepdims=True) + EPS) * g


def _dot(a, b):
    return jnp.dot(a, b, preferred_element_type=F32)


def _dot_nt(a, b):
    return lax.dot_general(a, b, (((1,), (1,)), ((), ())), preferred_element_type=F32)


def _params(*sem):
    return pltpu.CompilerParams(dimension_semantics=sem, vmem_limit_bytes=VMEM_LIMIT)


def _ffn_kernel(x_ref, g_ref, wg_ref, wu_ref, wd_ref, gf_ref, o_ref, h_scr, acc_scr, *, final_norm):
    j = pl.program_id(1)

    @pl.when(j == 0)
    def _():
        h_scr[...] = _rms(x_ref[...], g_ref[...]).astype(BF16)
        acc_scr[...] = jnp.zeros_like(acc_scr)

    h = h_scr[...]
    a = _dot(h, wg_ref[...])
    u = _dot(h, wu_ref[...])
    act = a * jax.nn.sigmoid(a) * u
    acc_scr[...] += _dot(act.astype(BF16), wd_ref[...])

    @pl.when(j == pl.num_programs(1) - 1)
    def _():
        y = x_ref[...] + 0.5 * acc_scr[...]
        if final_norm:
            y = _rms(y, gf_ref[...])
        o_ref[...] = y


def _ffn(x2d, g, wg, wu, wd, gf, *, final_norm):
    n, d = x2d.shape
    dff = wg.shape[1]
    tm = min(1024, n)
    tf = 256
    return pl.pallas_call(
        functools.partial(_ffn_kernel, final_norm=final_norm),
        grid=(n // tm, dff // tf),
        in_specs=[
            pl.BlockSpec((tm, d), lambda i, j: (i, 0)),
            pl.BlockSpec((1, d), lambda i, j: (0, 0)),
            pl.BlockSpec((d, tf), lambda i, j: (0, j)),
            pl.BlockSpec((d, tf), lambda i, j: (0, j)),
            pl.BlockSpec((tf, d), lambda i, j: (j, 0)),
            pl.BlockSpec((1, d), lambda i, j: (0, 0)),
        ],
        out_specs=pl.BlockSpec((tm, d), lambda i, j: (i, 0)),
        out_shape=jax.ShapeDtypeStruct((n, d), F32),
        scratch_shapes=[pltpu.VMEM((tm, d), BF16), pltpu.VMEM((tm, d), F32)],
        compiler_params=_params("parallel", "arbitrary"),
        name="ffn",
    )(x2d, g, wg, wu, wd, gf)


def _expand_head_pairs(r, heads, kv_heads):
    lane = lax.broadcasted_iota(jnp.int32, (r.shape[0], LANES), 1)
    low = lane < HEAD_DIM
    tiles = []
    for p in range(heads // 2):
        pair = r[:, p * LANES:(p + 1) * LANES]
        swapped = pltpu.roll(pair, HEAD_DIM, 1)
        if (2 * p) // (heads // kv_heads) == 0:
            tiles += [jnp.where(low, pair, 0.0), jnp.where(low, swapped, 0.0)]
        else:
            tiles += [jnp.where(low, 0.0, swapped), jnp.where(low, 0.0, pair)]
    return tiles


def _compact_head_pairs(tiles, heads, kv_heads):
    lane = lax.broadcasted_iota(jnp.int32, tiles[0].shape, 1)
    low = lane < HEAD_DIM
    out = []
    for p in range(heads // 2):
        a, b = tiles[2 * p], tiles[2 * p + 1]
        if (2 * p) // (heads // kv_heads) == 0:
            out.append(jnp.where(low, a, pltpu.roll(b, HEAD_DIM, 1)))
        else:
            out.append(jnp.where(low, pltpu.roll(a, HEAD_DIM, 1), b))
    return jnp.concatenate(out, axis=1)


def _inproj_kernel(x_ref, g_ref, wqa_ref, wqb_ref, wkv_ref, wc_ref, wg_ref,
                   qa_ref, qb_ref, kv_ref, kc_ref, vc_ref, gn_ref, *, seq_len):
    tm = x_ref.shape[0]
    h = _rms(x_ref[...], g_ref[...]).astype(BF16)
    for w_ref, q_ref, heads, kv_heads in ((wqa_ref, qa_ref, NSA_HEADS, NSA_GROUPS),
                                          (wqb_ref, qb_ref, SWA_HEADS, SWA_KV_HEADS)):
        tiles = _expand_head_pairs(_dot(h, w_ref[...]), heads, kv_heads)
        for hh, t in enumerate(tiles):
            q_ref[:, hh * LANES:(hh + 1) * LANES] = t.astype(BF16)
    kv = _dot(h, wkv_ref[...]).astype(BF16)
    pos = (pl.program_id(0) * tm + lax.broadcasted_iota(jnp.int32, (tm, LANES), 0)) % seq_len
    lane = lax.broadcasted_iota(jnp.int32, (tm, LANES), 1)
    block_onehot = jnp.where(lane == pos // SEL_BLOCK, 1.0, 0.0).astype(BF16)
    ones = jnp.ones((tm, LANES), BF16)
    for i in range(kv.shape[1] // LANES):
        kv_ref[:, 2 * i * LANES:(2 * i + 1) * LANES] = kv[:, i * LANES:(i + 1) * LANES]
        kv_ref[:, (2 * i + 1) * LANES:(2 * i + 2) * LANES] = block_onehot if i % 2 == 0 else ones
    c = _dot(h, wc_ref[...])
    kc_ref[...] = c[:, :LANES]
    vc_ref[...] = c[:, LANES:]
    gn_ref[...] = jax.nn.sigmoid(_dot(h, wg_ref[...]))


def _inproj(x2d, g, wqa, wqb, wkv, wc, wg, *, seq_len):
    n, d = x2d.shape
    kv_w = 2 * wkv.shape[1]
    tm = min(512, n)
    full = lambda w: pl.BlockSpec(w.shape, lambda i: (0, 0))
    row = lambda width: pl.BlockSpec((tm, width), lambda i: (i, 0))
    return pl.pallas_call(
        functools.partial(_inproj_kernel, seq_len=seq_len),
        grid=(n // tm,),
        in_specs=[row(d), full(g), full(wqa), full(wqb), full(wkv), full(wc), full(wg)],
        out_specs=[row(2 * wqa.shape[1]), row(2 * wqb.shape[1]), row(kv_w), row(LANES), row(LANES),
                   row(LANES)],
        out_shape=[
            jax.ShapeDtypeStruct((n, 2 * wqa.shape[1]), BF16),
            jax.ShapeDtypeStruct((n, 2 * wqb.shape[1]), BF16),
            jax.ShapeDtypeStruct((n, kv_w), BF16),
            jax.ShapeDtypeStruct((n, LANES), F32),
            jax.ShapeDtypeStruct((n, LANES), F32),
            jax.ShapeDtypeStruct((n, LANES), F32),
        ],
        compiler_params=_params("parallel"),
        name="inproj",
    )(x2d, g, wqa, wqb, wkv, wc, wg)


def _compress_kernel(ak_ref, av_ref, pek_ref, pev_ref, wka_ref, wkb_ref, wva_ref, wvb_ref,
                     w2k_ref, w2v_ref, ov_ref, kc_ref, vc_ref):
    def one(a_ref, pe_ref, wa_ref, wb_ref, w2_ref):
        a = a_ref[0]
        rows = a.shape[0]
        u = _dot((a + pe_ref[0:1, :]).astype(BF16), wa_ref[...])
        v = _dot((a + pe_ref[1:2, :]).astype(BF16), wb_ref[...])
        hid = u + pltpu.roll(v, rows - 1, 0)
        hid = jax.nn.gelu(hid, approximate=True)
        return _dot(hid.astype(BF16), w2_ref[...]).astype(BF16)

    kc_ref[0] = one(ak_ref, pek_ref, wka_ref, wkb_ref, w2k_ref)
    vc_ref[0, :, :LANES] = one(av_ref, pev_ref, wva_ref, wvb_ref, w2v_ref)
    vc_ref[0, :, LANES:2 * LANES] = ov_ref[...]
    vc_ref[0, :, 2 * LANES:] = jnp.ones(ov_ref.shape, BF16)


def _compress(ak, av, pek, pev, wka, wkb, wva, wvb, w2k, w2v, ov):
    b, ncp, width = ak.shape
    full = lambda w: pl.BlockSpec(w.shape, lambda i: (0,) * w.ndim)
    per_b = lambda w: pl.BlockSpec((1, ncp, w), lambda i: (i, 0, 0))
    return pl.pallas_call(
        _compress_kernel,
        grid=(b,),
        in_specs=[per_b(width), per_b(width), full(pek), full(pev), full(wka), full(wkb),
                  full(wva), full(wvb), full(w2k), full(w2v), full(ov)],
        out_specs=[per_b(LANES), per_b(3 * LANES)],
        out_shape=[jax.ShapeDtypeStruct((b, ncp, LANES), BF16),
                   jax.ShapeDtypeStruct((b, ncp, 3 * LANES), BF16)],
        compiler_params=_params("parallel"),
        name="compress",
    )(ak, av, pek, pev, wka, wkb, wva, wvb, w2k, w2v, ov)


def _stack_heads(q, heads):
    return jnp.concatenate([q[:, h * LANES:(h + 1) * LANES] for h in range(heads)], axis=0)


def _window_rows(ref, s0, window_chunks, fallback):
    parts = []
    for r in range(window_chunks):
        start = s0 + (r - (window_chunks - 1)) * TQ
        start = jnp.where(start < 0, fallback, start)
        parts.append(ref[0, pl.ds(pl.multiple_of(start, TQ), TQ), :])
    return jnp.concatenate(parts, axis=0)


def _lanes(a, b):
    return jnp.concatenate([a, b], axis=1)


def _per_head(tiles_per_group):
    return jnp.concatenate([tiles_per_group[h // HEADS_PER_GROUP] for h in range(NSA_HEADS)], axis=0)


def _mix_kernel(qa_ref, qb_ref, gn_ref, kc_ref, vc_ref, ks_ref, vs_ref, kw_ref, vw_ref, kb_ref, vb_ref,
                ca_ref, tw0_ref, tw1_ref, ts_ref, tb_ref, sink_ref,
                oa_ref, ob_ref, m_scr, acc_scr, sfa_scr, sfb_scr, *, n_sel):
    c = pl.program_id(1)
    s0 = c * TQ
    rows = NSA_HEADS * TQ
    ncp = kc_ref.shape[1]
    nsp = LANES
    win = NSA_WIN_CHUNKS * TQ

    q = _stack_heads(qa_ref[0], NSA_HEADS)
    t_col = s0 + lax.broadcasted_iota(jnp.int32, (rows, 1), 0) % TQ

    n_i = lax.broadcasted_iota(jnp.int32, (ncp, LANES), 0)
    lane_c = lax.broadcasted_iota(jnp.int32, (ncp, LANES), 1)
    w0 = s0 // CMP_STRIDE - CMP_WIN_BACK
    slot = jnp.where(lane_c < CMP_BLOCK, lane_c, lane_c - CMP_BLOCK)
    place = ((lane_c < 2 * CMP_BLOCK) & (n_i == w0 + slot)) | (
        (lane_c == 2 * CMP_BLOCK) & (n_i >= w0 + CMP_BLOCK))
    kc_aug = _lanes(kc_ref[0], jnp.where(place, 1.0, 0.0).astype(BF16))
    s_c = _dot_nt(_lanes(q, ca_ref[...]), kc_aug)
    e_c = jnp.exp2(s_c - jnp.max(s_c, axis=-1, keepdims=True)).astype(BF16)
    r_c = _dot(e_c, vc_ref[0])
    inv_c = jnp.where(t_col >= CMP_BLOCK - 1, 1.0 / r_c[:, 2 * LANES:], 0.0)
    o_c = r_c[:, :LANES] * inv_c
    imp_h = r_c[:, LANES:2 * LANES] * inv_c

    t_q = s0 + lax.broadcasted_iota(jnp.int32, (TQ, 1), 0)
    cur_q = t_q // SEL_BLOCK
    blk_l = lax.broadcasted_iota(jnp.int32, (TQ, nsp), 1)
    forced = (blk_l == 0) | (blk_l == cur_q) | (blk_l == cur_q - 1)
    causal_blk = blk_l * SEL_BLOCK <= t_q
    cand = causal_blk & jnp.logical_not(forced)
    n_sweeps = min(SEL_TOPN, n_sel) - 3
    blk_s = lax.broadcasted_iota(jnp.int32, (nsp, TQ), 0).astype(F32)
    first_blk = (s0 - NSA_WINDOW) // SEL_BLOCK

    near_q, far_q = [], []
    for g in range(NSA_GROUPS):
        base = g * HEADS_PER_GROUP * TQ
        imp = imp_h[base:base + TQ]
        for h in range(1, HEADS_PER_GROUP):
            imp = imp + imp_h[base + h * TQ:base + (h + 1) * TQ]
        v_t = jnp.where(cand, imp, -1.0).T
        for _ in range(n_sweeps):
            mx = jnp.max(v_t, axis=0, keepdims=True)
            first = jnp.min(jnp.where(v_t == mx, blk_s, float(nsp)), axis=0, keepdims=True)
            v_t = jnp.where(blk_s == first, -2.0, v_t)
        chosen = (forced | (v_t.T == -2.0)) & causal_blk
        near_q.append(jnp.where(chosen, 0.0, NEG).astype(BF16))
        far_q.append(jnp.where(chosen & (blk_l < first_blk), 0.0, NEG).astype(BF16))

    s_len = ks_ref.shape[1]
    spare = s_len - TQ
    causal_q = _per_head([jnp.where(causal_blk, 0.0, NEG).astype(BF16)] * NSA_GROUPS)

    def add_tiles(s, first_ref, last_ref):
        first = s[:, :TQ] if first_ref is None else s[:, :TQ] + first_ref[...]
        return jnp.concatenate([first, s[:, TQ:win - 2 * TQ], s[:, win - 2 * TQ:] + last_ref[...]], axis=1)

    s_w = _dot_nt(_lanes(q, causal_q), _window_rows(kw_ref, s0, NSA_WIN_CHUNKS, spare))
    s_w = add_tiles(s_w, tw0_ref, tw1_ref)
    e_w = jnp.exp2((s_w - jnp.max(s_w, axis=-1, keepdims=True)).astype(BF16))
    r_w = _dot(e_w, _window_rows(vw_ref, s0, NSA_WIN_CHUNKS, spare))
    o_w = r_w[:, :LANES] / r_w[:, LANES:]

    s_n = _dot_nt(_lanes(q, _per_head(near_q)), _window_rows(ks_ref, s0, NSA_WIN_CHUNKS, spare))
    s_n = add_tiles(s_n, None, ts_ref)
    m_n = jnp.max(s_n, axis=-1, keepdims=True)
    m_scr[...] = m_n
    acc_scr[...] = _dot(jnp.exp2((s_n - m_n).astype(BF16)),
                        _window_rows(vs_ref, s0, NSA_WIN_CHUNKS, spare))

    q_far = _lanes(q, _per_head(far_q))
    n_far = jnp.maximum(s0 - NSA_WINDOW + FAR_CHUNK - 1, 0) // FAR_CHUNK
    last_chunk = s_len // FAR_CHUNK - 1
    group_rows = [slice(g * rows // NSA_GROUPS, (g + 1) * rows // NSA_GROUPS) for g in range(NSA_GROUPS)]

    def far_rows(j):
        return pl.ds(pl.multiple_of(jnp.minimum(j, last_chunk) * FAR_CHUNK, FAR_CHUNK), FAR_CHUNK)

    def far_logits(j, buf):
        for r in group_rows:
            buf[r, :] = _dot_nt(q_far[r], ks_ref[0, far_rows(j), :])

    def far_update(buf, j):
        for r in group_rows:
            s_f = buf[r, :]
            m_old = m_scr[r, :]
            m_new = jnp.maximum(m_old, jnp.max(s_f, axis=-1, keepdims=True))
            e_f = jnp.exp2((s_f - m_new).astype(BF16))
            acc_scr[r, :] = jnp.exp2(m_old - m_new) * acc_scr[r, :] + _dot(e_f, vs_ref[0, far_rows(j), :])
            m_scr[r, :] = m_new

    far_logits(0, sfa_scr)

    def far_step(i, carry):
        far_logits(2 * i + 1, sfb_scr)
        far_update(sfa_scr, 2 * i)
        far_logits(2 * i + 2, sfa_scr)
        far_update(sfb_scr, 2 * i + 1)
        return carry

    lax.fori_loop(0, n_far // 2, far_step, 0)

    @pl.when(n_far % 2 == 1)
    def _():
        far_update(sfa_scr, n_far - 1)

    o_s = acc_scr[:, :LANES] / acc_scr[:, LANES:]

    gn = gn_ref[0]
    tiles = []
    for h in range(NSA_HEADS):
        r = slice(h * TQ, (h + 1) * TQ)
        tiles.append(gn[:, h:h + 1] * o_c[r] + gn[:, NSA_HEADS + h:NSA_HEADS + h + 1] * o_s[r]
                     + gn[:, 2 * NSA_HEADS + h:2 * NSA_HEADS + h + 1] * o_w[r])
    oa_ref[0] = _compact_head_pairs(tiles, NSA_HEADS, NSA_GROUPS).astype(BF16)

    s_b = _dot_nt(_lanes(_stack_heads(qb_ref[0], SWA_HEADS), causal_q),
                  _window_rows(kb_ref, s0, SWA_WIN_CHUNKS, spare)) + tb_ref[...]
    sink = sink_ref[...]
    m_b = jnp.maximum(jnp.max(s_b, axis=-1, keepdims=True), sink)
    r_b = _dot(jnp.exp2((s_b - m_b).astype(BF16)), _window_rows(vb_ref, s0, SWA_WIN_CHUNKS, spare))
    o_b = r_b[:, :LANES] / (r_b[:, LANES:] + jnp.exp2(sink - m_b))
    ob_ref[0] = _compact_head_pairs([o_b[h * TQ:(h + 1) * TQ] for h in range(SWA_HEADS)],
                                    SWA_HEADS, SWA_KV_HEADS).astype(BF16)


def _mix(qa, qb, gn, kc, vc, kv, ca, tw0, tw1, ts, tb, sink, *, n_sel):
    b, s, qw = qa.shape
    ncp = kc.shape[1]
    rows = NSA_HEADS * TQ
    once = pl.Buffered(1)
    full = lambda w: pl.BlockSpec(w.shape, lambda i, j: (0,) * w.ndim, pipeline_mode=once)
    per_b = lambda blk, col, w: pl.BlockSpec((1, blk, w), lambda i, j: (i, 0, col), pipeline_mode=once)
    tile = lambda w: pl.BlockSpec((1, TQ, w), lambda i, j: (i, j, 0))
    return pl.pallas_call(
        functools.partial(_mix_kernel, n_sel=n_sel),
        grid=(b, s // TQ),
        in_specs=[tile(qw), tile(qw), tile(LANES), per_b(ncp, 0, LANES), per_b(ncp, 0, 3 * LANES)]
        + [per_b(s, col, 2 * LANES) for col in range(6)]
        + [full(ca), full(tw0), full(tw1), full(ts), full(tb), full(sink)],
        out_specs=[tile(qw // 2), tile(qw // 2)],
        out_shape=[jax.ShapeDtypeStruct((b, s, qw // 2), BF16)] * 2,
        scratch_shapes=[pltpu.VMEM((rows, 1), F32), pltpu.VMEM((rows, 2 * LANES), F32),
                        pltpu.VMEM((rows, FAR_CHUNK), F32), pltpu.VMEM((rows, FAR_CHUNK), F32)],
        compiler_params=_params("parallel", "arbitrary"),
        name="mix",
    )(qa, qb, gn, kc, vc, kv, kv, kv, kv, kv, kv, ca, tw0, tw1, ts, tb, sink)


def _merge_kernel(x_ref, g_ref, oa_ref, ob_ref, wga_ref, wgb_ref, wua_ref, wub_ref, wo_ref, o_ref):
    x = x_ref[...]
    h = _rms(x, g_ref[...]).astype(BF16)
    merged = (jax.nn.sigmoid(_dot(h, wga_ref[...])) * _dot(oa_ref[...], wua_ref[...])
              + jax.nn.sigmoid(_dot(h, wgb_ref[...])) * _dot(ob_ref[...], wub_ref[...]))
    o_ref[...] = x + _dot(merged.astype(BF16), wo_ref[...])


def _merge(x2d, g, oa, ob, wga, wgb, wua, wub, wo):
    n, d = x2d.shape
    tm = min(512, n)
    full = lambda w: pl.BlockSpec(w.shape, lambda i: (0, 0))
    row = lambda width: pl.BlockSpec((tm, width), lambda i: (i, 0))
    return pl.pallas_call(
        _merge_kernel,
        grid=(n // tm,),
        in_specs=[row(d), full(g), row(oa.shape[1]), row(ob.shape[1]), full(wga), full(wgb),
                  full(wua), full(wub), full(wo)],
        out_specs=row(d),
        out_shape=jax.ShapeDtypeStruct((n, d), F32),
        compiler_params=_params("parallel"),
        name="merge",
    )(x2d, g, oa, ob, wga, wgb, wua, wub, wo)


def _memkv_kernel(m_ref, g_ref, w_ref, o_ref):
    o_ref[...] = _dot(_rms(m_ref[...], g_ref[...]).astype(BF16), w_ref[...]).astype(BF16)


def _memkv(m2d, g, w):
    n, d = m2d.shape
    tm = min(512, n)
    return pl.pallas_call(
        _memkv_kernel,
        grid=(n // tm,),
        in_specs=[pl.BlockSpec((tm, d), lambda i: (i, 0)), pl.BlockSpec(g.shape, lambda i: (0, 0)),
                  pl.BlockSpec(w.shape, lambda i: (0, 0))],
        out_specs=pl.BlockSpec((tm, w.shape[1]), lambda i: (i, 0)),
        out_shape=jax.ShapeDtypeStruct((n, w.shape[1]), BF16),
        compiler_params=_params("parallel"),
        name="memkv",
    )(m2d, g, w)


def _xattn_kernel(x_ref, g_ref, kv_ref, wq_ref, wo_ref, o_ref):
    x = x_ref[0]
    d = x.shape[1]
    hd = d // XATTN_HEADS
    q = _dot(_rms(x, g_ref[...]).astype(BF16), wq_ref[...]).astype(BF16)
    outs = []
    for h in range(XATTN_HEADS):
        k = kv_ref[0, :, h * hd:(h + 1) * hd]
        v = kv_ref[0, :, d + h * hd:d + (h + 1) * hd]
        s = _dot_nt(q[:, h * hd:(h + 1) * hd], k)
        e = jnp.exp2(s - jnp.max(s, axis=-1, keepdims=True))
        outs.append((_dot(e.astype(BF16), v) / jnp.sum(e, axis=-1, keepdims=True)).astype(BF16))
    o_ref[0] = x + _dot(jnp.concatenate(outs, axis=1), wo_ref[...])


def _xattn(x3d, g, kv, wq, wo):
    b, s, d = x3d.shape
    tm = min(512, s)
    full = lambda w: pl.BlockSpec(w.shape, lambda i, j: (0, 0))
    return pl.pallas_call(
        _xattn_kernel,
        grid=(b, s // tm),
        in_specs=[pl.BlockSpec((1, tm, d), lambda i, j: (i, j, 0)), full(g),
                  pl.BlockSpec((1,) + kv.shape[1:], lambda i, j: (i, 0, 0)), full(wq), full(wo)],
        out_specs=pl.BlockSpec((1, tm, d), lambda i, j: (i, j, 0)),
        out_shape=jax.ShapeDtypeStruct((b, s, d), F32),
        compiler_params=_params("parallel", "parallel"),
        name="xattn",
    )(x3d, g, kv, wq, wo)


def _rel_bucket(dist):
    dist = jnp.maximum(dist, 0)
    max_exact = REL_BUCKETS // 2
    d = jnp.maximum(dist, 1).astype(F32)
    large = max_exact + (jnp.log(d / max_exact) / math.log(REL_MAX_DIST / max_exact)
                         * (REL_BUCKETS - max_exact)).astype(jnp.int32)
    large = jnp.minimum(large, REL_BUCKETS - 1)
    return jnp.where(dist < max_exact, dist, large)


def _bias_of_dist(bias, dist):
    bucket = _rel_bucket(dist)[None]
    out = jnp.zeros((bias.shape[1],) + dist.shape, F32)
    for k in range(REL_BUCKETS):
        out = jnp.where(bucket == k, bias[k].reshape((-1,) + (1,) * dist.ndim), out)
    return out


def _bias_tile(bias, shift, window_chunks, hi):
    win = window_chunks * TQ
    dist = jnp.arange(TQ)[:, None] + (window_chunks - 1) * TQ - jnp.arange(win)[None, :]
    vals = _bias_of_dist(bias, dist) - shift[:, None, None]
    return jnp.where((dist >= 0) & (dist < hi), vals, NEG).reshape(-1, win)


def _cmp_tables(w1, pe):
    half = CMP_BLOCK // 2
    eye = jnp.eye(NSA_GROUPS, dtype=F32)
    w = w1.reshape(2, half, HEAD_DIM, CMP_HIDDEN)
    wexp = jnp.einsum('aldh,gk->algdkh', w, eye)
    wexp = wexp.reshape(2, half * NSA_GROUPS * HEAD_DIM, NSA_GROUPS * CMP_HIDDEN)
    pexp = jnp.broadcast_to(pe.reshape(2, half, 1, HEAD_DIM), (2, half, NSA_GROUPS, HEAD_DIM))
    return wexp[0].astype(BF16), wexp[1].astype(BF16), pexp.reshape(2, -1)


def _block_diag2(w):
    z = jnp.zeros_like(w)
    return jnp.concatenate([jnp.concatenate([w, z], axis=1), jnp.concatenate([z, w], axis=1)], axis=0)


def kernel(x, mem, norm_ffn1, w1_gate, w1_up, w1_down, norm_mix, w_in, cmp_pe_k, cmp_w1_k, cmp_w2_k,
           cmp_pe_v, cmp_w1_v, cmp_w2_v, attn_sinks, rel_bias, w_up_a, w_up_b, w_out, norm_xattn,
           norm_mem, w_xq, w_xkv, w_xo, norm_ffn2, w2_gate, w2_up, w2_down, norm_final):
    b, s, d = x.shape
    n = b * s
    assert norm_ffn1.shape[0] == 1, "single-layer kernel"
    assert s % FAR_CHUNK == 0 and s >= 2 * NSA_WINDOW
    n_cmp = (s - CMP_BLOCK) // CMP_STRIDE + 1
    ncp = s // CMP_STRIDE
    n_sel = s // SEL_BLOCK
    assert n_sel <= LANES, "selection blocks must fit the 128 mask lanes"
    bf = lambda w: w.astype(BF16)
    scale = HEAD_DIM ** -0.5
    log2e = math.log2(math.e)

    wi = w_in[0]
    nq = NSA_HEADS * HEAD_DIM
    nkv = NSA_GROUPS * HEAD_DIM
    o_g = nq + 6 * nkv
    o_qb = o_g + 3 * NSA_HEADS
    o_kb = o_qb + SWA_HEADS * HEAD_DIM
    o_ga = o_kb + 2 * SWA_KV_HEADS * HEAD_DIM
    wqa = bf(wi[:, :nq] * (scale * log2e))
    wqb = bf(wi[:, o_qb:o_kb] * (scale * log2e))
    wkv = bf(jnp.concatenate([wi[:, nq + 2 * nkv:o_g], wi[:, o_kb:o_ga]], axis=1))
    wc = bf(wi[:, nq:nq + 2 * nkv])
    wgn = bf(jnp.pad(wi[:, o_g:o_qb], ((0, 0), (0, LANES - 3 * NSA_HEADS))))
    wga = bf(wi[:, o_ga:o_ga + d])
    wgb = bf(wi[:, o_ga + d:o_ga + 2 * d])
    wka, wkb, pek = _cmp_tables(cmp_w1_k[0], cmp_pe_k[0])
    wva, wvb, pev = _cmp_tables(cmp_w1_v[0], cmp_pe_v[0])
    w2k = bf(_block_diag2(cmp_w2_k[0]))
    w2v = bf(_block_diag2(cmp_w2_v[0]))

    bias_a = rel_bias[:, :NSA_HEADS] * log2e
    bias_b = rel_bias[:, NSA_HEADS:] * log2e
    far_bias = _bias_of_dist(bias_a, jnp.full((1,), REL_MAX_DIST, jnp.int32))[:, 0]
    no_shift = jnp.zeros((SWA_HEADS,), F32)
    tw = _bias_tile(bias_a, far_bias, NSA_WIN_CHUNKS, NSA_WINDOW)
    ts = _bias_tile(bias_a, far_bias, NSA_WIN_CHUNKS, NSA_WIN_CHUNKS * TQ)
    tb = _bias_tile(bias_b, no_shift, SWA_WIN_CHUNKS, SWA_WINDOW)
    dist_c = (jnp.arange(TQ)[:, None] + (CMP_WIN_BACK * CMP_STRIDE - (CMP_BLOCK - 1))
              - CMP_STRIDE * jnp.arange(CMP_BLOCK)[None, :])
    dcv = jnp.where(dist_c < REL_MAX_DIST, _bias_of_dist(bias_a, dist_c) - far_bias[:, None, None], 0.0)
    dcv = jnp.where(dist_c >= 0, dcv, NEG).reshape(NSA_HEADS * TQ, CMP_BLOCK)
    dc_hi = dcv.astype(BF16)
    dc_lo = (dcv - dc_hi.astype(F32)).astype(BF16)
    after = jnp.full((NSA_HEADS * TQ, 1), NEG, BF16)
    ca = jnp.concatenate([dc_hi, dc_lo, after,
                          jnp.zeros((NSA_HEADS * TQ, LANES - 2 * CMP_BLOCK - 1), BF16)], axis=1)
    sink = jnp.repeat(attn_sinks[0] * log2e, TQ)[:, None]

    cs = np.arange(ncp)[:, None] * CMP_STRIDE
    ss = np.arange(LANES)[None, :] * SEL_BLOCK
    ovl = np.maximum(np.minimum(cs + CMP_BLOCK, ss + SEL_BLOCK) - np.maximum(cs, ss), 0) / CMP_BLOCK
    ovl[n_cmp:] = 0.0
    ov = jnp.asarray(ovl, dtype=BF16)

    row = lambda g: g.reshape(1, -1)
    x1 = _ffn(x.reshape(n, d), row(norm_ffn1[0]), bf(w1_gate[0]), bf(w1_up[0]), bf(w1_down[0]),
              row(norm_final), final_norm=False)
    qa, qb, kv, kcf, vcf, gn = _inproj(x1, row(norm_mix[0]), wqa, wqb, wkv, wc, wgn, seq_len=s)
    tok_w = CMP_STRIDE * LANES
    kc, vc = _compress(kcf.reshape(b, ncp, tok_w), vcf.reshape(b, ncp, tok_w), pek, pev,
                       wka, wkb, wva, wvb, w2k, w2v, ov)
    oa, ob = _mix(qa.reshape(b, s, -1), qb.reshape(b, s, -1), gn.reshape(b, s, LANES), kc, vc,
                  kv.reshape(b, s, -1), ca, tw[:, :TQ], tw[:, -2 * TQ:], ts[:, -2 * TQ:], tb, sink,
                  n_sel=n_sel)
    x2 = _merge(x1, row(norm_mix[0]), oa.reshape(n, -1), ob.reshape(n, -1), wga, wgb,
                bf(w_up_a[0]), bf(w_up_b[0]), bf(w_out[0]))
    mkv = _memkv(mem.reshape(-1, d), row(norm_mem[0]), bf(w_xkv[0]))
    x3 = _xattn(x2.reshape(b, s, d), row(norm_xattn[0]), mkv.reshape(b, -1, 2 * d),
                bf(w_xq[0] * ((d // XATTN_HEADS) ** -0.5 * log2e)), bf(w_xo[0]))
    out = _ffn(x3.reshape(n, d), row(norm_ffn2[0]), bf(w2_gate[0]), bf(w2_up[0]), bf(w2_down[0]),
               row(norm_final), final_norm=True)
    return out.reshape(b, s, d)
```

```python
import functools
import math

import numpy as np
import jax
import jax.numpy as jnp
from jax import lax
from jax.experimental import pallas as pl
from jax.experimental.pallas import tpu as pltpu

F32 = jnp.float32
BF16 = jnp.bfloat16

HEAD_DIM = 64
NSA_HEADS = 8
NSA_GROUPS = 2
HEADS_PER_GROUP = NSA_HEADS // NSA_GROUPS
CMP_BLOCK = 32
CMP_STRIDE = 16
CMP_HIDDEN = 256
SEL_BLOCK = 64
SEL_TOPN = 16
NSA_WINDOW = 512
SWA_HEADS = 8
SWA_KV_HEADS = 2
SWA_WINDOW = 128
REL_BUCKETS = 32
REL_MAX_DIST = 128
XATTN_HEADS = 4
EPS = 1e-6
NEG = -1e30

LANES = 128
MXU_WIDTH = 256
FFN_ROWS = 1024
ROW_TILE = 512
XATTN_ROWS = 1024
TQ = 128
FAR_CHUNK = 512
NSA_WIN_CHUNKS = NSA_WINDOW // TQ + 1
SWA_WIN_CHUNKS = SWA_WINDOW // TQ + 1
CMP_WIN_BACK = 16
VMEM_LIMIT = 56 * 1024 * 1024


def _rms(x, g):
    return x * lax.rsqrt(jnp.mean(x * x, axis=-1, keepdims=True) + EPS) * g


def _dot(a, b):
    return jnp.dot(a, b, preferred_element_type=F32)


def _dot_nt(a, b):
    return lax.dot_general(a, b, (((1,), (1,)), ((), ())), preferred_element_type=F32)


def _params(*sem):
    return pltpu.CompilerParams(dimension_semantics=sem, vmem_limit_bytes=VMEM_LIMIT)


def _ffn_kernel(x_ref, g_ref, wg_ref, wu_ref, wd_ref, gf_ref, o_ref, h_scr, acc_scr, *, final_norm):
    j = pl.program_id(1)

    @pl.when(j == 0)
    def _():
        h_scr[...] = _rms(x_ref[...], g_ref[...]).astype(BF16)
        acc_scr[...] = jnp.zeros_like(acc_scr)

    h = h_scr[...]
    a = _dot(h, wg_ref[...])
    u = _dot(h, wu_ref[...])
    act = a * jax.nn.sigmoid(a) * u
    acc_scr[...] += _dot(act.astype(BF16), wd_ref[...])

    @pl.when(j == pl.num_programs(1) - 1)
    def _():
        y = x_ref[...] + 0.5 * acc_scr[...]
        if final_norm:
            y = _rms(y, gf_ref[...])
        o_ref[...] = y


def _ffn(x2d, g, wg, wu, wd, gf, *, final_norm):
    n, d = x2d.shape
    dff = wg.shape[1]
    tm = min(FFN_ROWS, n)
    tf = MXU_WIDTH
    return pl.pallas_call(
        functools.partial(_ffn_kernel, final_norm=final_norm),
        grid=(n // tm, dff // tf),
        in_specs=[
            pl.BlockSpec((tm, d), lambda i, j: (i, 0)),
            pl.BlockSpec((1, d), lambda i, j: (0, 0)),
            pl.BlockSpec((d, tf), lambda i, j: (0, j)),
            pl.BlockSpec((d, tf), lambda i, j: (0, j)),
            pl.BlockSpec((tf, d), lambda i, j: (j, 0)),
            pl.BlockSpec((1, d), lambda i, j: (0, 0)),
        ],
        out_specs=pl.BlockSpec((tm, d), lambda i, j: (i, 0)),
        out_shape=jax.ShapeDtypeStruct((n, d), F32),
        scratch_shapes=[pltpu.VMEM((tm, d), BF16), pltpu.VMEM((tm, d), F32)],
        compiler_params=_params("parallel", "arbitrary"),
        name="ffn",
    )(x2d, g, wg, wu, wd, gf)


def _expand_head_pairs(r, heads, kv_heads):
    lane = lax.broadcasted_iota(jnp.int32, (r.shape[0], LANES), 1)
    low = lane < HEAD_DIM
    tiles = []
    for p in range(heads // 2):
        pair = r[:, p * LANES:(p + 1) * LANES]
        swapped = pltpu.roll(pair, HEAD_DIM, 1)
        if (2 * p) // (heads // kv_heads) == 0:
            tiles += [jnp.where(low, pair, 0.0), jnp.where(low, swapped, 0.0)]
        else:
            tiles += [jnp.where(low, 0.0, swapped), jnp.where(low, 0.0, pair)]
    return tiles


def _compact_head_pairs(tiles, heads, kv_heads):
    lane = lax.broadcasted_iota(jnp.int32, tiles[0].shape, 1)
    low = lane < HEAD_DIM
    out = []
    for p in range(heads // 2):
        a, b = tiles[2 * p], tiles[2 * p + 1]
        if (2 * p) // (heads // kv_heads) == 0:
            out.append(jnp.where(low, a, pltpu.roll(b, HEAD_DIM, 1)))
        else:
            out.append(jnp.where(low, pltpu.roll(a, HEAD_DIM, 1), b))
    return jnp.concatenate(out, axis=1)


def _inproj_kernel(x_ref, g_ref, wqa_ref, wqb_ref, wkv_ref, wc_ref, wg_ref,
                   qa_ref, qb_ref, kv_ref, kc_ref, vc_ref, gn_ref, *, seq_len):
    tm = x_ref.shape[0]
    h = _rms(x_ref[...], g_ref[...]).astype(BF16)
    for w_ref, q_ref, heads, kv_heads in ((wqa_ref, qa_ref, NSA_HEADS, NSA_GROUPS),
                                          (wqb_ref, qb_ref, SWA_HEADS, SWA_KV_HEADS)):
        tiles = _expand_head_pairs(_dot(h, w_ref[...]), heads, kv_heads)
        for hh, t in enumerate(tiles):
            q_ref[:, hh * LANES:(hh + 1) * LANES] = t.astype(BF16)
    kv = _dot(h, wkv_ref[...]).astype(BF16)
    pos = (pl.program_id(0) * tm + lax.broadcasted_iota(jnp.int32, (tm, LANES), 0)) % seq_len
    lane = lax.broadcasted_iota(jnp.int32, (tm, LANES), 1)
    block_onehot = jnp.where(lane == pos // SEL_BLOCK, 1.0, 0.0).astype(BF16)
    ones = jnp.ones((tm, LANES), BF16)
    for i in range(kv.shape[1] // LANES):
        kv_ref[:, 2 * i * LANES:(2 * i + 1) * LANES] = kv[:, i * LANES:(i + 1) * LANES]
        kv_ref[:, (2 * i + 1) * LANES:(2 * i + 2) * LANES] = block_onehot if i % 2 == 0 else ones
    c = _dot(h, wc_ref[...])
    kc_ref[...] = c[:, :LANES]
    vc_ref[...] = c[:, LANES:]
    gn_ref[...] = jax.nn.sigmoid(_dot(h, wg_ref[...]))


def _inproj(x2d, g, wqa, wqb, wkv, wc, wg, *, seq_len):
    n, d = x2d.shape
    kv_w = 2 * wkv.shape[1]
    tm = min(ROW_TILE, n)
    full = lambda w: pl.BlockSpec(w.shape, lambda i: (0, 0))
    row = lambda width: pl.BlockSpec((tm, width), lambda i: (i, 0))
    return pl.pallas_call(
        functools.partial(_inproj_kernel, seq_len=seq_len),
        grid=(n // tm,),
        in_specs=[row(d), full(g), full(wqa), full(wqb), full(wkv), full(wc), full(wg)],
        out_specs=[row(2 * wqa.shape[1]), row(2 * wqb.shape[1]), row(kv_w), row(LANES), row(LANES),
                   row(LANES)],
        out_shape=[
            jax.ShapeDtypeStruct((n, 2 * wqa.shape[1]), BF16),
            jax.ShapeDtypeStruct((n, 2 * wqb.shape[1]), BF16),
            jax.ShapeDtypeStruct((n, kv_w), BF16),
            jax.ShapeDtypeStruct((n, LANES), F32),
            jax.ShapeDtypeStruct((n, LANES), F32),
            jax.ShapeDtypeStruct((n, LANES), F32),
        ],
        compiler_params=_params("parallel"),
        name="inproj",
    )(x2d, g, wqa, wqb, wkv, wc, wg)


def _compress_kernel(ak_ref, av_ref, pek_ref, pev_ref, wka_ref, wkb_ref, wva_ref, wvb_ref,
                     w2k_ref, w2v_ref, ov_ref, kc_ref, vc_ref):
    rows = kc_ref.shape[1]

    def one(a_ref, pe_ref, wa_ref, wb_ref, w2_ref):
        u = v = None
        for l in range(CMP_STRIDE):
            tok = a_ref[0, pl.ds(l, rows, stride=CMP_STRIDE), :]
            lanes = slice(l * LANES, (l + 1) * LANES)
            du = _dot((tok + pe_ref[0:1, lanes]).astype(BF16), wa_ref[lanes, :])
            dv = _dot((tok + pe_ref[1:2, lanes]).astype(BF16), wb_ref[lanes, :])
            u = du if u is None else u + du
            v = dv if v is None else v + dv
        hid = u + pltpu.roll(v, rows - 1, 0)
        hid = jax.nn.gelu(hid, approximate=True)
        return _dot(hid.astype(BF16), w2_ref[...]).astype(BF16)

    kc_ref[0] = one(ak_ref, pek_ref, wka_ref, wkb_ref, w2k_ref)
    vc_ref[0, :, :LANES] = one(av_ref, pev_ref, wva_ref, wvb_ref, w2v_ref)
    vc_ref[0, :, LANES:2 * LANES] = ov_ref[...]
    vc_ref[0, :, 2 * LANES:] = jnp.ones(ov_ref.shape, BF16)


def _compress(ak, av, pek, pev, wka, wkb, wva, wvb, w2k, w2v, ov):
    b, s, width = ak.shape
    ncp = s // CMP_STRIDE
    full = lambda w: pl.BlockSpec(w.shape, lambda i: (0,) * w.ndim)
    per_b = lambda w: pl.BlockSpec((1, ncp, w), lambda i: (i, 0, 0))
    tokens = pl.BlockSpec((1, s, width), lambda i: (i, 0, 0))
    return pl.pallas_call(
        _compress_kernel,
        grid=(b,),
        in_specs=[tokens, tokens, full(pek), full(pev), full(wka), full(wkb),
                  full(wva), full(wvb), full(w2k), full(w2v), full(ov)],
        out_specs=[per_b(LANES), per_b(3 * LANES)],
        out_shape=[jax.ShapeDtypeStruct((b, ncp, LANES), BF16),
                   jax.ShapeDtypeStruct((b, ncp, 3 * LANES), BF16)],
        compiler_params=_params("parallel"),
        name="compress",
    )(ak, av, pek, pev, wka, wkb, wva, wvb, w2k, w2v, ov)


def _stack_heads(q, heads):
    return jnp.concatenate([q[:, h * LANES:(h + 1) * LANES] for h in range(heads)], axis=0)


def _window_rows(ref, s0, window_chunks, fallback):
    parts = []
    for r in range(window_chunks):
        start = s0 + (r - (window_chunks - 1)) * TQ
        start = jnp.where(start < 0, fallback, start)
        parts.append(ref[0, pl.ds(pl.multiple_of(start, TQ), TQ), :])
    return jnp.concatenate(parts, axis=0)


def _lanes(a, b):
    return jnp.concatenate([a, b], axis=1)


def _per_head(tiles_per_group):
    return jnp.concatenate([tiles_per_group[h // HEADS_PER_GROUP] for h in range(NSA_HEADS)], axis=0)


def _mix_kernel(qa_ref, qb_ref, gn_ref, kc_ref, vc_ref, ks_ref, vs_ref, kw_ref, vw_ref, kb_ref, vb_ref,
                ca_ref, tw0_ref, tw1_ref, ts_ref, tb_ref, sink_ref,
                oa_ref, ob_ref, m_scr, acc_scr, sfa_scr, sfb_scr, *, n_sel):
    c = pl.program_id(1)
    s0 = c * TQ
    rows = NSA_HEADS * TQ
    ncp = kc_ref.shape[1]
    nsp = LANES
    win = NSA_WIN_CHUNKS * TQ

    q = _stack_heads(qa_ref[0], NSA_HEADS)
    t_col = s0 + lax.broadcasted_iota(jnp.int32, (rows, 1), 0) % TQ

    n_i = lax.broadcasted_iota(jnp.int32, (ncp, LANES), 0)
    lane_c = lax.broadcasted_iota(jnp.int32, (ncp, LANES), 1)
    w0 = s0 // CMP_STRIDE - CMP_WIN_BACK
    slot = jnp.where(lane_c < CMP_BLOCK, lane_c, lane_c - CMP_BLOCK)
    place = ((lane_c < 2 * CMP_BLOCK) & (n_i == w0 + slot)) | (
        (lane_c == 2 * CMP_BLOCK) & (n_i >= w0 + CMP_BLOCK))
    kc_aug = _lanes(kc_ref[0], jnp.where(place, 1.0, 0.0).astype(BF16))
    s_c = _dot_nt(_lanes(q, ca_ref[...]), kc_aug)
    e_c = jnp.exp2(s_c - jnp.max(s_c, axis=-1, keepdims=True)).astype(BF16)
    r_c = _dot(e_c, vc_ref[0])
    inv_c = jnp.where(t_col >= CMP_BLOCK - 1, 1.0 / r_c[:, 2 * LANES:], 0.0)
    o_c = r_c[:, :LANES] * inv_c
    imp_h = r_c[:, LANES:2 * LANES] * inv_c

    t_q = s0 + lax.broadcasted_iota(jnp.int32, (TQ, 1), 0)
    cur_q = t_q // SEL_BLOCK
    blk_l = lax.broadcasted_iota(jnp.int32, (TQ, nsp), 1)
    forced = (blk_l == 0) | (blk_l == cur_q) | (blk_l == cur_q - 1)
    causal_blk = blk_l * SEL_BLOCK <= t_q
    cand = causal_blk & jnp.logical_not(forced)
    n_sweeps = min(SEL_TOPN, n_sel) - 3
    blk_s = lax.broadcasted_iota(jnp.int32, (nsp, TQ), 0).astype(F32)
    first_blk = (s0 - NSA_WINDOW) // SEL_BLOCK

    near_q, far_q = [], []
    for g in range(NSA_GROUPS):
        base = g * HEADS_PER_GROUP * TQ
        imp = imp_h[base:base + TQ]
        for h in range(1, HEADS_PER_GROUP):
            imp = imp + imp_h[base + h * TQ:base + (h + 1) * TQ]
        v_t = jnp.where(cand, imp, -1.0).T
        for _ in range(n_sweeps):
            mx = jnp.max(v_t, axis=0, keepdims=True)
            first = jnp.min(jnp.where(v_t == mx, blk_s, float(nsp)), axis=0, keepdims=True)
            v_t = jnp.where(blk_s == first, -2.0, v_t)
        chosen = (forced | (v_t.T == -2.0)) & causal_blk
        near_q.append(jnp.where(chosen, 0.0, NEG).astype(BF16))
        far_q.append(jnp.where(chosen & (blk_l < first_blk), 0.0, NEG).astype(BF16))

    s_len = ks_ref.shape[1]
    spare = s_len - TQ
    causal_q = _per_head([jnp.where(causal_blk, 0.0, NEG).astype(BF16)] * NSA_GROUPS)

    def add_tiles(s, first_ref, last_ref):
        first = s[:, :TQ] if first_ref is None else s[:, :TQ] + first_ref[...]
        return jnp.concatenate([first, s[:, TQ:win - 2 * TQ], s[:, win - 2 * TQ:] + last_ref[...]], axis=1)

    s_w = _dot_nt(_lanes(q, causal_q), _window_rows(kw_ref, s0, NSA_WIN_CHUNKS, spare))
    s_w = add_tiles(s_w, tw0_ref, tw1_ref)
    e_w = jnp.exp2((s_w - jnp.max(s_w, axis=-1, keepdims=True)).astype(BF16))
    r_w = _dot(e_w, _window_rows(vw_ref, s0, NSA_WIN_CHUNKS, spare))
    o_w = r_w[:, :LANES] / r_w[:, LANES:]

    s_n = _dot_nt(_lanes(q, _per_head(near_q)), _window_rows(ks_ref, s0, NSA_WIN_CHUNKS, spare))
    s_n = add_tiles(s_n, None, ts_ref)
    m_n = jnp.max(s_n, axis=-1, keepdims=True)
    m_scr[...] = m_n
    acc_scr[...] = _dot(jnp.exp2((s_n - m_n).astype(BF16)),
                        _window_rows(vs_ref, s0, NSA_WIN_CHUNKS, spare))

    q_far = _lanes(q, _per_head(far_q))
    n_far = jnp.maximum(s0 - NSA_WINDOW + FAR_CHUNK - 1, 0) // FAR_CHUNK
    last_chunk = s_len // FAR_CHUNK - 1
    group_rows = [slice(g * rows // NSA_GROUPS, (g + 1) * rows // NSA_GROUPS) for g in range(NSA_GROUPS)]

    def far_rows(j):
        return pl.ds(pl.multiple_of(jnp.minimum(j, last_chunk) * FAR_CHUNK, FAR_CHUNK), FAR_CHUNK)

    def far_logits(j, buf):
        for r in group_rows:
            buf[r, :] = _dot_nt(q_far[r], ks_ref[0, far_rows(j), :])

    def far_update(buf, j):
        for r in group_rows:
            s_f = buf[r, :]
            m_old = m_scr[r, :]
            m_new = jnp.maximum(m_old, jnp.max(s_f, axis=-1, keepdims=True))
            e_f = jnp.exp2((s_f - m_new).astype(BF16))
            acc_scr[r, :] = jnp.exp2(m_old - m_new) * acc_scr[r, :] + _dot(e_f, vs_ref[0, far_rows(j), :])
            m_scr[r, :] = m_new

    far_logits(0, sfa_scr)

    def far_step(i, carry):
        far_logits(2 * i + 1, sfb_scr)
        far_update(sfa_scr, 2 * i)
        far_logits(2 * i + 2, sfa_scr)
        far_update(sfb_scr, 2 * i + 1)
        return carry

    lax.fori_loop(0, n_far // 2, far_step, 0)

    @pl.when(n_far % 2 == 1)
    def _():
        far_update(sfa_scr, n_far - 1)

    o_s = acc_scr[:, :LANES] / acc_scr[:, LANES:]

    gn = gn_ref[0]
    tiles = []
    for h in range(NSA_HEADS):
        r = slice(h * TQ, (h + 1) * TQ)
        tiles.append(gn[:, h:h + 1] * o_c[r] + gn[:, NSA_HEADS + h:NSA_HEADS + h + 1] * o_s[r]
                     + gn[:, 2 * NSA_HEADS + h:2 * NSA_HEADS + h + 1] * o_w[r])
    oa_ref[0] = _compact_head_pairs(tiles, NSA_HEADS, NSA_GROUPS).astype(BF16)

    s_b = _dot_nt(_lanes(_stack_heads(qb_ref[0], SWA_HEADS), causal_q),
                  _window_rows(kb_ref, s0, SWA_WIN_CHUNKS, spare)) + tb_ref[...]
    sink = sink_ref[...]
    m_b = jnp.maximum(jnp.max(s_b, axis=-1, keepdims=True), sink)
    r_b = _dot(jnp.exp2((s_b - m_b).astype(BF16)), _window_rows(vb_ref, s0, SWA_WIN_CHUNKS, spare))
    o_b = r_b[:, :LANES] / (r_b[:, LANES:] + jnp.exp2(sink - m_b))
    ob_ref[0] = _compact_head_pairs([o_b[h * TQ:(h + 1) * TQ] for h in range(SWA_HEADS)],
                                    SWA_HEADS, SWA_KV_HEADS).astype(BF16)


def _mix(qa, qb, gn, kc, vc, kv, ca, tw0, tw1, ts, tb, sink, *, n_sel):
    b, s, qw = qa.shape
    ncp = kc.shape[1]
    rows = NSA_HEADS * TQ
    once = pl.Buffered(1)
    full = lambda w: pl.BlockSpec(w.shape, lambda i, j: (0,) * w.ndim, pipeline_mode=once)
    per_b = lambda blk, col, w: pl.BlockSpec((1, blk, w), lambda i, j: (i, 0, col), pipeline_mode=once)
    tile = lambda w: pl.BlockSpec((1, TQ, w), lambda i, j: (i, j, 0))
    return pl.pallas_call(
        functools.partial(_mix_kernel, n_sel=n_sel),
        grid=(b, s // TQ),
        in_specs=[tile(qw), tile(qw), tile(LANES), per_b(ncp, 0, LANES), per_b(ncp, 0, 3 * LANES)]
        + [per_b(s, col, 2 * LANES) for col in range(6)]
        + [full(ca), full(tw0), full(tw1), full(ts), full(tb), full(sink)],
        out_specs=[tile(qw // 2), tile(qw // 2)],
        out_shape=[jax.ShapeDtypeStruct((b, s, qw // 2), BF16)] * 2,
        scratch_shapes=[pltpu.VMEM((rows, 1), F32), pltpu.VMEM((rows, 2 * LANES), F32),
                        pltpu.VMEM((rows, FAR_CHUNK), F32), pltpu.VMEM((rows, FAR_CHUNK), F32)],
        compiler_params=_params("parallel", "arbitrary"),
        name="mix",
    )(qa, qb, gn, kc, vc, kv, kv, kv, kv, kv, kv, ca, tw0, tw1, ts, tb, sink)


def _merge_kernel(x_ref, g_ref, oa_ref, ob_ref, wga_ref, wgb_ref, wua_ref, wub_ref, wo_ref, o_ref):
    x = x_ref[...]
    h = _rms(x, g_ref[...]).astype(BF16)
    merged = (jax.nn.sigmoid(_dot(h, wga_ref[...])) * _dot(oa_ref[...], wua_ref[...])
              + jax.nn.sigmoid(_dot(h, wgb_ref[...])) * _dot(ob_ref[...], wub_ref[...]))
    o_ref[...] = x + _dot(merged.astype(BF16), wo_ref[...])


def _merge(x2d, g, oa, ob, wga, wgb, wua, wub, wo):
    n, d = x2d.shape
    tm = min(ROW_TILE, n)
    full = lambda w: pl.BlockSpec(w.shape, lambda i: (0, 0))
    row = lambda width: pl.BlockSpec((tm, width), lambda i: (i, 0))
    return pl.pallas_call(
        _merge_kernel,
        grid=(n // tm,),
        in_specs=[row(d), full(g), row(oa.shape[1]), row(ob.shape[1]), full(wga), full(wgb),
                  full(wua), full(wub), full(wo)],
        out_specs=row(d),
        out_shape=jax.ShapeDtypeStruct((n, d), F32),
        compiler_params=_params("parallel"),
        name="merge",
    )(x2d, g, oa, ob, wga, wgb, wua, wub, wo)


def _memkv_kernel(m_ref, g_ref, w_ref, o_ref):
    o_ref[...] = _dot(_rms(m_ref[...], g_ref[...]).astype(BF16), w_ref[...]).astype(BF16)


def _memkv(m2d, g, w):
    n, d = m2d.shape
    tm = min(ROW_TILE, n)
    return pl.pallas_call(
        _memkv_kernel,
        grid=(n // tm,),
        in_specs=[pl.BlockSpec((tm, d), lambda i: (i, 0)), pl.BlockSpec(g.shape, lambda i: (0, 0)),
                  pl.BlockSpec(w.shape, lambda i: (0, 0))],
        out_specs=pl.BlockSpec((tm, w.shape[1]), lambda i: (i, 0)),
        out_shape=jax.ShapeDtypeStruct((n, w.shape[1]), BF16),
        compiler_params=_params("parallel"),
        name="memkv",
    )(m2d, g, w)


def _xattn_kernel(x_ref, g_ref, kv_ref, wq_ref, wo_ref, o_ref):
    x = x_ref[0]
    d = x.shape[1]
    hd = d // XATTN_HEADS
    q = _dot(_rms(x, g_ref[...]).astype(BF16), wq_ref[...]).astype(BF16)
    outs = []
    for h in range(XATTN_HEADS):
        k = kv_ref[0, :, h * hd:(h + 1) * hd]
        v = kv_ref[0, :, d + h * hd:d + (h + 1) * hd]
        s = _dot_nt(q[:, h * hd:(h + 1) * hd], k)
        e = jnp.exp2(s - jnp.max(s, axis=-1, keepdims=True))
        outs.append((_dot(e.astype(BF16), v) / jnp.sum(e, axis=-1, keepdims=True)).astype(BF16))
    o_ref[0] = x + _dot(jnp.concatenate(outs, axis=1), wo_ref[...])


def _xattn(x3d, g, kv, wq, wo):
    b, s, d = x3d.shape
    tm = min(XATTN_ROWS, s)
    full = lambda w: pl.BlockSpec(w.shape, lambda i, j: (0, 0))
    return pl.pallas_call(
        _xattn_kernel,
        grid=(b, s // tm),
        in_specs=[pl.BlockSpec((1, tm, d), lambda i, j: (i, j, 0)), full(g),
                  pl.BlockSpec((1,) + kv.shape[1:], lambda i, j: (i, 0, 0)), full(wq), full(wo)],
        out_specs=pl.BlockSpec((1, tm, d), lambda i, j: (i, j, 0)),
        out_shape=jax.ShapeDtypeStruct((b, s, d), F32),
        compiler_params=_params("parallel", "parallel"),
        name="xattn",
    )(x3d, g, kv, wq, wo)


def _rel_bucket(dist):
    dist = jnp.maximum(dist, 0)
    max_exact = REL_BUCKETS // 2
    d = jnp.maximum(dist, 1).astype(F32)
    large = max_exact + (jnp.log(d / max_exact) / math.log(REL_MAX_DIST / max_exact)
                         * (REL_BUCKETS - max_exact)).astype(jnp.int32)
    large = jnp.minimum(large, REL_BUCKETS - 1)
    return jnp.where(dist < max_exact, dist, large)


def _bias_of_dist(bias, dist):
    bucket = _rel_bucket(dist)[None]
    out = jnp.zeros((bias.shape[1],) + dist.shape, F32)
    for k in range(REL_BUCKETS):
        out = jnp.where(bucket == k, bias[k].reshape((-1,) + (1,) * dist.ndim), out)
    return out


def _bias_tile(bias, shift, window_chunks, hi):
    win = window_chunks * TQ
    dist = jnp.arange(TQ)[:, None] + (window_chunks - 1) * TQ - jnp.arange(win)[None, :]
    vals = _bias_of_dist(bias, dist) - shift[:, None, None]
    return jnp.where((dist >= 0) & (dist < hi), vals, NEG).reshape(-1, win)


def _cmp_tables(w1, pe):
    half = CMP_BLOCK // 2
    eye = jnp.eye(NSA_GROUPS, dtype=F32)
    w = w1.reshape(2, half, HEAD_DIM, CMP_HIDDEN)
    wexp = jnp.einsum('aldh,gk->algdkh', w, eye)
    wexp = wexp.reshape(2, half * NSA_GROUPS * HEAD_DIM, NSA_GROUPS * CMP_HIDDEN)
    pexp = jnp.broadcast_to(pe.reshape(2, half, 1, HEAD_DIM), (2, half, NSA_GROUPS, HEAD_DIM))
    return wexp[0].astype(BF16), wexp[1].astype(BF16), pexp.reshape(2, -1)


def _block_diag2(w):
    z = jnp.zeros_like(w)
    return jnp.concatenate([jnp.concatenate([w, z], axis=1), jnp.concatenate([z, w], axis=1)], axis=0)


def kernel(x, mem, norm_ffn1, w1_gate, w1_up, w1_down, norm_mix, w_in, cmp_pe_k, cmp_w1_k, cmp_w2_k,
           cmp_pe_v, cmp_w1_v, cmp_w2_v, attn_sinks, rel_bias, w_up_a, w_up_b, w_out, norm_xattn,
           norm_mem, w_xq, w_xkv, w_xo, norm_ffn2, w2_gate, w2_up, w2_down, norm_final):
    b, s, d = x.shape
    n = b * s
    assert norm_ffn1.shape[0] == 1, "single-layer kernel"
    assert s % FAR_CHUNK == 0 and s >= 2 * NSA_WINDOW
    n_cmp = (s - CMP_BLOCK) // CMP_STRIDE + 1
    ncp = s // CMP_STRIDE
    n_sel = s // SEL_BLOCK
    assert n_sel <= LANES, "selection blocks must fit the 128 mask lanes"
    bf = lambda w: w.astype(BF16)
    scale = HEAD_DIM ** -0.5
    log2e = math.log2(math.e)

    wi = w_in[0]
    nq = NSA_HEADS * HEAD_DIM
    nkv = NSA_GROUPS * HEAD_DIM
    o_g = nq + 6 * nkv
    o_qb = o_g + 3 * NSA_HEADS
    o_kb = o_qb + SWA_HEADS * HEAD_DIM
    o_ga = o_kb + 2 * SWA_KV_HEADS * HEAD_DIM
    wqa = bf(wi[:, :nq] * (scale * log2e))
    wqb = bf(wi[:, o_qb:o_kb] * (scale * log2e))
    wkv = bf(jnp.concatenate([wi[:, nq + 2 * nkv:o_g], wi[:, o_kb:o_ga]], axis=1))
    wc = bf(wi[:, nq:nq + 2 * nkv])
    wgn = bf(jnp.pad(wi[:, o_g:o_qb], ((0, 0), (0, LANES - 3 * NSA_HEADS))))
    wga = bf(wi[:, o_ga:o_ga + d])
    wgb = bf(wi[:, o_ga + d:o_ga + 2 * d])
    wka, wkb, pek = _cmp_tables(cmp_w1_k[0], cmp_pe_k[0])
    wva, wvb, pev = _cmp_tables(cmp_w1_v[0], cmp_pe_v[0])
    w2k = bf(_block_diag2(cmp_w2_k[0]))
    w2v = bf(_block_diag2(cmp_w2_v[0]))

    bias_a = rel_bias[:, :NSA_HEADS] * log2e
    bias_b = rel_bias[:, NSA_HEADS:] * log2e
    far_bias = _bias_of_dist(bias_a, jnp.full((1,), REL_MAX_DIST, jnp.int32))[:, 0]
    no_shift = jnp.zeros((SWA_HEADS,), F32)
    tw = _bias_tile(bias_a, far_bias, NSA_WIN_CHUNKS, NSA_WINDOW)
    ts = _bias_tile(bias_a, far_bias, NSA_WIN_CHUNKS, NSA_WIN_CHUNKS * TQ)
    tb = _bias_tile(bias_b, no_shift, SWA_WIN_CHUNKS, SWA_WINDOW)
    dist_c = (jnp.arange(TQ)[:, None] + (CMP_WIN_BACK * CMP_STRIDE - (CMP_BLOCK - 1))
              - CMP_STRIDE * jnp.arange(CMP_BLOCK)[None, :])
    dcv = jnp.where(dist_c < REL_MAX_DIST, _bias_of_dist(bias_a, dist_c) - far_bias[:, None, None], 0.0)
    dcv = jnp.where(dist_c >= 0, dcv, NEG).reshape(NSA_HEADS * TQ, CMP_BLOCK)
    dc_hi = dcv.astype(BF16)
    dc_lo = (dcv - dc_hi.astype(F32)).astype(BF16)
    after = jnp.full((NSA_HEADS * TQ, 1), NEG, BF16)
    ca = jnp.concatenate([dc_hi, dc_lo, after,
                          jnp.zeros((NSA_HEADS * TQ, LANES - 2 * CMP_BLOCK - 1), BF16)], axis=1)
    sink = jnp.repeat(attn_sinks[0] * log2e, TQ)[:, None]

    cs = np.arange(ncp)[:, None] * CMP_STRIDE
    ss = np.arange(LANES)[None, :] * SEL_BLOCK
    ovl = np.maximum(np.minimum(cs + CMP_BLOCK, ss + SEL_BLOCK) - np.maximum(cs, ss), 0) / CMP_BLOCK
    ovl[n_cmp:] = 0.0
    ov = jnp.asarray(ovl, dtype=BF16)

    row = lambda g: g.reshape(1, -1)
    x1 = _ffn(x.reshape(n, d), row(norm_ffn1[0]), bf(w1_gate[0]), bf(w1_up[0]), bf(w1_down[0]),
              row(norm_final), final_norm=False)
    qa, qb, kv, kcf, vcf, gn = _inproj(x1, row(norm_mix[0]), wqa, wqb, wkv, wc, wgn, seq_len=s)
    kc, vc = _compress(kcf.reshape(b, s, LANES), vcf.reshape(b, s, LANES), pek, pev,
                       wka, wkb, wva, wvb, w2k, w2v, ov)
    oa, ob = _mix(qa.reshape(b, s, -1), qb.reshape(b, s, -1), gn.reshape(b, s, LANES), kc, vc,
                  kv.reshape(b, s, -1), ca, tw[:, :TQ], tw[:, -2 * TQ:], ts[:, -2 * TQ:], tb, sink,
                  n_sel=n_sel)
    x2 = _merge(x1, row(norm_mix[0]), oa.reshape(n, -1), ob.reshape(n, -1), wga, wgb,
                bf(w_up_a[0]), bf(w_up_b[0]), bf(w_out[0]))
    mkv = _memkv(mem.reshape(-1, d), row(norm_mem[0]), bf(w_xkv[0]))
    x3 = _xattn(x2.reshape(b, s, d), row(norm_xattn[0]), mkv.reshape(b, -1, 2 * d),
                bf(w_xq[0] * ((d // XATTN_HEADS) ** -0.5 * log2e)), bf(w_xo[0]))
    out = _ffn(x3.reshape(n, d), row(norm_ffn2[0]), bf(w2_gate[0]), bf(w2_up[0]), bf(w2_down[0]),
               row(norm_final), final_norm=True)
    return out.reshape(b, s, d)
```

```python
import functools
import math

import numpy as np
import jax
import jax.numpy as jnp
from jax import lax
from jax.experimental import pallas as pl
from jax.experimental.pallas import tpu as pltpu

F32 = jnp.float32
BF16 = jnp.bfloat16

HEAD_DIM = 64
NSA_HEADS = 8
NSA_GROUPS = 2
HEADS_PER_GROUP = NSA_HEADS // NSA_GROUPS
CMP_BLOCK = 32
CMP_STRIDE = 16
CMP_HIDDEN = 256
SEL_BLOCK = 64
SEL_TOPN = 16
NSA_WINDOW = 512
SWA_HEADS = 8
SWA_KV_HEADS = 2
SWA_WINDOW = 128
REL_BUCKETS = 32
REL_MAX_DIST = 128
XATTN_HEADS = 4
EPS = 1e-6
NEG = -1e30

LANES = 128
MXU_WIDTH = 256
FFN_ROWS = 1024
ROW_TILE = 512
XATTN_ROWS = 1024
TQ = 128
FAR_CHUNK = 512
NSA_WIN_CHUNKS = NSA_WINDOW // TQ + 1
SWA_WIN_CHUNKS = SWA_WINDOW // TQ + 1
CMP_WIN_BACK = 16
VMEM_LIMIT = 56 * 1024 * 1024


def _rms(x, g):
    return x * lax.rsqrt(jnp.mean(x * x, axis=-1, keepdims=True) + EPS) * g


def _dot(a, b):
    return jnp.dot(a, b, preferred_element_type=F32)


def _dot_nt(a, b):
    return lax.dot_general(a, b, (((1,), (1,)), ((), ())), preferred_element_type=F32)


def _params(*sem):
    return pltpu.CompilerParams(dimension_semantics=sem, vmem_limit_bytes=VMEM_LIMIT)


def _ffn_kernel(x_ref, g_ref, wg_ref, wu_ref, wd_ref, gf_ref, o_ref, h_scr, acc_scr, *, final_norm):
    j = pl.program_id(1)

    @pl.when(j == 0)
    def _():
        h_scr[...] = _rms(x_ref[...], g_ref[...]).astype(BF16)
        acc_scr[...] = jnp.zeros_like(acc_scr)

    h = h_scr[...]
    a = _dot(h, wg_ref[...])
    u = _dot(h, wu_ref[...])
    act = a * jax.nn.sigmoid(a) * u
    acc_scr[...] += _dot(act.astype(BF16), wd_ref[...])

    @pl.when(j == pl.num_programs(1) - 1)
    def _():
        y = x_ref[...] + 0.5 * acc_scr[...]
        if final_norm:
            y = _rms(y, gf_ref[...])
        o_ref[...] = y


def _ffn(x2d, g, wg, wu, wd, gf, *, final_norm):
    n, d = x2d.shape
    dff = wg.shape[1]
    tm = min(FFN_ROWS, n)
    tf = MXU_WIDTH
    return pl.pallas_call(
        functools.partial(_ffn_kernel, final_norm=final_norm),
        grid=(n // tm, dff // tf),
        in_specs=[
            pl.BlockSpec((tm, d), lambda i, j: (i, 0)),
            pl.BlockSpec((1, d), lambda i, j: (0, 0)),
            pl.BlockSpec((d, tf), lambda i, j: (0, j)),
            pl.BlockSpec((d, tf), lambda i, j: (0, j)),
            pl.BlockSpec((tf, d), lambda i, j: (j, 0)),
            pl.BlockSpec((1, d), lambda i, j: (0, 0)),
        ],
        out_specs=pl.BlockSpec((tm, d), lambda i, j: (i, 0)),
        out_shape=jax.ShapeDtypeStruct((n, d), F32),
        scratch_shapes=[pltpu.VMEM((tm, d), BF16), pltpu.VMEM((tm, d), F32)],
        compiler_params=_params("parallel", "arbitrary"),
        name="ffn",
    )(x2d, g, wg, wu, wd, gf)


def _expand_head_pairs(r, heads, kv_heads):
    lane = lax.broadcasted_iota(jnp.int32, (r.shape[0], LANES), 1)
    low = lane < HEAD_DIM
    tiles = []
    for p in range(heads // 2):
        pair = r[:, p * LANES:(p + 1) * LANES]
        swapped = pltpu.roll(pair, HEAD_DIM, 1)
        if (2 * p) // (heads // kv_heads) == 0:
            tiles += [jnp.where(low, pair, 0.0), jnp.where(low, swapped, 0.0)]
        else:
            tiles += [jnp.where(low, 0.0, swapped), jnp.where(low, 0.0, pair)]
    return tiles


def _compact_head_pairs(tiles, heads, kv_heads):
    lane = lax.broadcasted_iota(jnp.int32, tiles[0].shape, 1)
    low = lane < HEAD_DIM
    out = []
    for p in range(heads // 2):
        a, b = tiles[2 * p], tiles[2 * p + 1]
        if (2 * p) // (heads // kv_heads) == 0:
            out.append(jnp.where(low, a, pltpu.roll(b, HEAD_DIM, 1)))
        else:
            out.append(jnp.where(low, pltpu.roll(a, HEAD_DIM, 1), b))
    return jnp.concatenate(out, axis=1)


def _inproj_kernel(x_ref, g_ref, wqa_ref, wqb_ref, wkv_ref, wc_ref, wg_ref,
                   qa_ref, qb_ref, kv_ref, kc_ref, vc_ref, gn_ref, *, seq_len):
    tm = x_ref.shape[0]
    h = _rms(x_ref[...], g_ref[...]).astype(BF16)
    for w_ref, q_ref, heads, kv_heads in ((wqa_ref, qa_ref, NSA_HEADS, NSA_GROUPS),
                                          (wqb_ref, qb_ref, SWA_HEADS, SWA_KV_HEADS)):
        tiles = _expand_head_pairs(_dot(h, w_ref[...]), heads, kv_heads)
        for hh, t in enumerate(tiles):
            q_ref[:, hh * LANES:(hh + 1) * LANES] = t.astype(BF16)
    kv = _dot(h, wkv_ref[...]).astype(BF16)
    pos = (pl.program_id(0) * tm + lax.broadcasted_iota(jnp.int32, (tm, LANES), 0)) % seq_len
    lane = lax.broadcasted_iota(jnp.int32, (tm, LANES), 1)
    block_onehot = jnp.where(lane == pos // SEL_BLOCK, 1.0, 0.0).astype(BF16)
    ones = jnp.ones((tm, LANES), BF16)
    for i in range(kv.shape[1] // LANES):
        kv_ref[:, 2 * i * LANES:(2 * i + 1) * LANES] = kv[:, i * LANES:(i + 1) * LANES]
        kv_ref[:, (2 * i + 1) * LANES:(2 * i + 2) * LANES] = block_onehot if i % 2 == 0 else ones
    c = _dot(h, wc_ref[...])
    kc_ref[...] = c[:, :LANES]
    vc_ref[...] = c[:, LANES:]
    gn_ref[...] = jax.nn.sigmoid(_dot(h, wg_ref[...]))


def _inproj(x2d, g, wqa, wqb, wkv, wc, wg, *, seq_len):
    n, d = x2d.shape
    kv_w = 2 * wkv.shape[1]
    tm = min(ROW_TILE, n)
    full = lambda w: pl.BlockSpec(w.shape, lambda i: (0, 0))
    row = lambda width: pl.BlockSpec((tm, width), lambda i: (i, 0))
    return pl.pallas_call(
        functools.partial(_inproj_kernel, seq_len=seq_len),
        grid=(n // tm,),
        in_specs=[row(d), full(g), full(wqa), full(wqb), full(wkv), full(wc), full(wg)],
        out_specs=[row(2 * wqa.shape[1]), row(2 * wqb.shape[1]), row(kv_w), row(LANES), row(LANES),
                   row(LANES)],
        out_shape=[
            jax.ShapeDtypeStruct((n, 2 * wqa.shape[1]), BF16),
            jax.ShapeDtypeStruct((n, 2 * wqb.shape[1]), BF16),
            jax.ShapeDtypeStruct((n, kv_w), BF16),
            jax.ShapeDtypeStruct((n, LANES), F32),
            jax.ShapeDtypeStruct((n, LANES), F32),
            jax.ShapeDtypeStruct((n, LANES), F32),
        ],
        compiler_params=_params("parallel"),
        name="inproj",
    )(x2d, g, wqa, wqb, wkv, wc, wg)


def _compress_kernel(ak_ref, av_ref, pek_ref, pev_ref, wka_ref, wkb_ref, wva_ref, wvb_ref,
                     w2k_ref, w2v_ref, ov_ref, kc_ref, vc_ref):
    rows = kc_ref.shape[1]

    def one(a_ref, pe_ref, wa_ref, wb_ref, w2_ref):
        u = v = None
        for l in range(CMP_STRIDE):
            tok = a_ref[0, pl.ds(l, rows, stride=CMP_STRIDE), :]
            lanes = slice(l * LANES, (l + 1) * LANES)
            du = _dot((tok + pe_ref[0:1, lanes]).astype(BF16), wa_ref[lanes, :])
            dv = _dot((tok + pe_ref[1:2, lanes]).astype(BF16), wb_ref[lanes, :])
            u = du if u is None else u + du
            v = dv if v is None else v + dv
        hid = u + pltpu.roll(v, rows - 1, 0)
        hid = jax.nn.gelu(hid, approximate=True)
        return _dot(hid.astype(BF16), w2_ref[...]).astype(BF16)

    kc_ref[0] = one(ak_ref, pek_ref, wka_ref, wkb_ref, w2k_ref)
    vc_ref[0, :, :LANES] = one(av_ref, pev_ref, wva_ref, wvb_ref, w2v_ref)
    vc_ref[0, :, LANES:2 * LANES] = ov_ref[...]
    vc_ref[0, :, 2 * LANES:] = jnp.ones(ov_ref.shape, BF16)


def _compress(ak, av, pek, pev, wka, wkb, wva, wvb, w2k, w2v, ov):
    b, s, width = ak.shape
    ncp = s // CMP_STRIDE
    full = lambda w: pl.BlockSpec(w.shape, lambda i: (0,) * w.ndim)
    per_b = lambda w: pl.BlockSpec((1, ncp, w), lambda i: (i, 0, 0))
    tokens = pl.BlockSpec((1, s, width), lambda i: (i, 0, 0))
    return pl.pallas_call(
        _compress_kernel,
        grid=(b,),
        in_specs=[tokens, tokens, full(pek), full(pev), full(wka), full(wkb),
                  full(wva), full(wvb), full(w2k), full(w2v), full(ov)],
        out_specs=[per_b(LANES), per_b(3 * LANES)],
        out_shape=[jax.ShapeDtypeStruct((b, ncp, LANES), BF16),
                   jax.ShapeDtypeStruct((b, ncp, 3 * LANES), BF16)],
        compiler_params=_params("parallel"),
        name="compress",
    )(ak, av, pek, pev, wka, wkb, wva, wvb, w2k, w2v, ov)


def _stack_heads(q, heads):
    return jnp.concatenate([q[:, h * LANES:(h + 1) * LANES] for h in range(heads)], axis=0)


def _window_rows(ref, s0, window_chunks, fallback):
    parts = []
    for r in range(window_chunks):
        start = s0 + (r - (window_chunks - 1)) * TQ
        start = jnp.where(start < 0, fallback, start)
        parts.append(ref[0, pl.ds(pl.multiple_of(start, TQ), TQ), :])
    return jnp.concatenate(parts, axis=0)


def _lanes(a, b):
    return jnp.concatenate([a, b], axis=1)


def _per_head(tiles_per_group):
    return jnp.concatenate([tiles_per_group[h // HEADS_PER_GROUP] for h in range(NSA_HEADS)], axis=0)


def _mix_kernel(qa_ref, qb_ref, gn_ref, kc_ref, vc_ref, ks_ref, vs_ref, kw_ref, vw_ref, kb_ref, vb_ref,
                ca_ref, tw0_ref, tw1_ref, ts_ref, tb_ref, sink_ref,
                oa_ref, ob_ref, m_scr, acc_scr, sfa_scr, sfb_scr, *, n_sel):
    c = pl.program_id(1)
    s0 = c * TQ
    rows = NSA_HEADS * TQ
    ncp = kc_ref.shape[1]
    nsp = LANES
    win = NSA_WIN_CHUNKS * TQ

    q = _stack_heads(qa_ref[0], NSA_HEADS)
    t_col = s0 + lax.broadcasted_iota(jnp.int32, (rows, LANES), 0) % TQ

    n_i = lax.broadcasted_iota(jnp.int32, (ncp, LANES), 0)
    lane_c = lax.broadcasted_iota(jnp.int32, (ncp, LANES), 1)
    w0 = s0 // CMP_STRIDE - CMP_WIN_BACK
    slot = jnp.where(lane_c < CMP_BLOCK, lane_c, lane_c - CMP_BLOCK)
    place = ((lane_c < 2 * CMP_BLOCK) & (n_i == w0 + slot)) | (
        (lane_c == 2 * CMP_BLOCK) & (n_i >= w0 + CMP_BLOCK))
    kc_aug = _lanes(kc_ref[0], jnp.where(place, 1.0, 0.0).astype(BF16))
    s_c = _dot_nt(_lanes(q, ca_ref[...]), kc_aug)
    e_c = jnp.exp2(s_c - jnp.max(s_c, axis=-1, keepdims=True)).astype(BF16)
    r_c = _dot(e_c, vc_ref[0])
    inv_c = jnp.where(t_col >= CMP_BLOCK - 1, 1.0 / r_c[:, 2 * LANES:], 0.0)
    o_c = r_c[:, :LANES] * inv_c
    imp_h = r_c[:, LANES:2 * LANES] * inv_c

    t_q = s0 + lax.broadcasted_iota(jnp.int32, (TQ, 1), 0)
    cur_q = t_q // SEL_BLOCK
    blk_l = lax.broadcasted_iota(jnp.int32, (TQ, nsp), 1)
    forced = (blk_l == 0) | (blk_l == cur_q) | (blk_l == cur_q - 1)
    causal_blk = blk_l * SEL_BLOCK <= t_q
    cand = causal_blk & jnp.logical_not(forced)
    n_sweeps = min(SEL_TOPN, n_sel) - 3
    blk_s = lax.broadcasted_iota(jnp.int32, (nsp, TQ), 0).astype(F32)
    first_blk = (s0 - NSA_WINDOW) // SEL_BLOCK

    near_q, far_q = [], []
    for g in range(NSA_GROUPS):
        base = g * HEADS_PER_GROUP * TQ
        imp = imp_h[base:base + TQ]
        for h in range(1, HEADS_PER_GROUP):
            imp = imp + imp_h[base + h * TQ:base + (h + 1) * TQ]
        v_t = jnp.where(cand, imp, -1.0).T
        for _ in range(n_sweeps):
            mx = jnp.max(v_t, axis=0, keepdims=True)
            first = jnp.min(jnp.where(v_t == mx, blk_s, float(nsp)), axis=0, keepdims=True)
            v_t = jnp.where(blk_s == first, -2.0, v_t)
        chosen = (forced | (v_t.T == -2.0)) & causal_blk
        near_q.append(jnp.where(chosen, 0.0, NEG).astype(BF16))
        far_q.append(jnp.where(chosen & (blk_l < first_blk), 0.0, NEG).astype(BF16))

    s_len = ks_ref.shape[1]
    spare = s_len - TQ
    causal_q = _per_head([jnp.where(causal_blk, 0.0, NEG).astype(BF16)] * NSA_GROUPS)

    def add_tiles(s, first_ref, last_ref):
        first = s[:, :TQ] if first_ref is None else s[:, :TQ] + first_ref[...]
        return jnp.concatenate([first, s[:, TQ:win - 2 * TQ], s[:, win - 2 * TQ:] + last_ref[...]], axis=1)

    s_w = _dot_nt(_lanes(q, causal_q), _window_rows(kw_ref, s0, NSA_WIN_CHUNKS, spare))
    s_w = add_tiles(s_w, tw0_ref, tw1_ref)
    e_w = jnp.exp2((s_w - jnp.max(s_w, axis=-1, keepdims=True)).astype(BF16))
    r_w = _dot(e_w, _window_rows(vw_ref, s0, NSA_WIN_CHUNKS, spare))
    o_w = r_w[:, :LANES] / r_w[:, LANES:]

    s_n = _dot_nt(_lanes(q, _per_head(near_q)), _window_rows(ks_ref, s0, NSA_WIN_CHUNKS, spare))
    s_n = add_tiles(s_n, None, ts_ref)
    m_n = jnp.broadcast_to(jnp.max(s_n, axis=-1, keepdims=True), m_scr.shape)
    m_scr[...] = m_n
    acc_scr[...] = _dot(jnp.exp2((s_n - jnp.concatenate([m_n] * NSA_WIN_CHUNKS, axis=1)).astype(BF16)),
                        _window_rows(vs_ref, s0, NSA_WIN_CHUNKS, spare))

    q_far = _lanes(q, _per_head(far_q))
    n_far = jnp.maximum(s0 - NSA_WINDOW + FAR_CHUNK - 1, 0) // FAR_CHUNK
    last_chunk = s_len // FAR_CHUNK - 1
    group_rows = [slice(g * rows // NSA_GROUPS, (g + 1) * rows // NSA_GROUPS) for g in range(NSA_GROUPS)]

    def far_rows(j):
        return pl.ds(pl.multiple_of(jnp.minimum(j, last_chunk) * FAR_CHUNK, FAR_CHUNK), FAR_CHUNK)

    def far_logits(j, buf):
        for r in group_rows:
            buf[r, :] = _dot_nt(q_far[r], ks_ref[0, far_rows(j), :])

    def far_update(buf, j):
        for r in group_rows:
            s_f = buf[r, :]
            m_old = m_scr[r, :]
            m_new = jnp.maximum(m_old, jnp.broadcast_to(jnp.max(s_f, axis=-1, keepdims=True), m_old.shape))
            e_f = jnp.exp2((s_f - jnp.concatenate([m_new] * (FAR_CHUNK // LANES), axis=1)).astype(BF16))
            alpha = jnp.exp2(m_old - m_new)
            acc_scr[r, :] = _lanes(alpha, alpha) * acc_scr[r, :] + _dot(e_f, vs_ref[0, far_rows(j), :])
            m_scr[r, :] = m_new

    far_logits(0, sfa_scr)

    def far_step(i, carry):
        far_logits(2 * i + 1, sfb_scr)
        far_update(sfa_scr, 2 * i)
        far_logits(2 * i + 2, sfa_scr)
        far_update(sfb_scr, 2 * i + 1)
        return carry

    lax.fori_loop(0, n_far // 2, far_step, 0)

    @pl.when(n_far % 2 == 1)
    def _():
        far_update(sfa_scr, n_far - 1)

    o_s = acc_scr[:, :LANES] / acc_scr[:, LANES:]

    gn = gn_ref[0]
    tiles = []
    for h in range(NSA_HEADS):
        r = slice(h * TQ, (h + 1) * TQ)
        tiles.append(gn[:, h:h + 1] * o_c[r] + gn[:, NSA_HEADS + h:NSA_HEADS + h + 1] * o_s[r]
                     + gn[:, 2 * NSA_HEADS + h:2 * NSA_HEADS + h + 1] * o_w[r])
    oa_ref[0] = _compact_head_pairs(tiles, NSA_HEADS, NSA_GROUPS).astype(BF16)

    s_b = _dot_nt(_lanes(_stack_heads(qb_ref[0], SWA_HEADS), causal_q),
                  _window_rows(kb_ref, s0, SWA_WIN_CHUNKS, spare)) + tb_ref[...]
    sink = sink_ref[...]
    m_b = jnp.maximum(jnp.broadcast_to(jnp.max(s_b, axis=-1, keepdims=True), sink.shape), sink)
    e_b = jnp.exp2((s_b - jnp.concatenate([m_b] * SWA_WIN_CHUNKS, axis=1)).astype(BF16))
    r_b = _dot(e_b, _window_rows(vb_ref, s0, SWA_WIN_CHUNKS, spare))
    o_b = r_b[:, :LANES] / (r_b[:, LANES:] + jnp.exp2(sink - m_b))
    ob_ref[0] = _compact_head_pairs([o_b[h * TQ:(h + 1) * TQ] for h in range(SWA_HEADS)],
                                    SWA_HEADS, SWA_KV_HEADS).astype(BF16)


def _mix(qa, qb, gn, kc, vc, kv, ca, tw0, tw1, ts, tb, sink, *, n_sel):
    b, s, qw = qa.shape
    ncp = kc.shape[1]
    rows = NSA_HEADS * TQ
    once = pl.Buffered(1)
    full = lambda w: pl.BlockSpec(w.shape, lambda i, j: (0,) * w.ndim, pipeline_mode=once)
    per_b = lambda blk, col, w: pl.BlockSpec((1, blk, w), lambda i, j: (i, 0, col), pipeline_mode=once)
    tile = lambda w: pl.BlockSpec((1, TQ, w), lambda i, j: (i, j, 0))
    return pl.pallas_call(
        functools.partial(_mix_kernel, n_sel=n_sel),
        grid=(b, s // TQ),
        in_specs=[tile(qw), tile(qw), tile(LANES), per_b(ncp, 0, LANES), per_b(ncp, 0, 3 * LANES)]
        + [per_b(s, col, 2 * LANES) for col in range(6)]
        + [full(ca), full(tw0), full(tw1), full(ts), full(tb), full(sink)],
        out_specs=[tile(qw // 2), tile(qw // 2)],
        out_shape=[jax.ShapeDtypeStruct((b, s, qw // 2), BF16)] * 2,
        scratch_shapes=[pltpu.VMEM((rows, LANES), F32), pltpu.VMEM((rows, 2 * LANES), F32),
                        pltpu.VMEM((rows, FAR_CHUNK), F32), pltpu.VMEM((rows, FAR_CHUNK), F32)],
        compiler_params=_params("parallel", "arbitrary"),
        name="mix",
    )(qa, qb, gn, kc, vc, kv, kv, kv, kv, kv, kv, ca, tw0, tw1, ts, tb, sink)


def _merge_kernel(x_ref, g_ref, oa_ref, ob_ref, wga_ref, wgb_ref, wua_ref, wub_ref, wo_ref, o_ref):
    x = x_ref[...]
    h = _rms(x, g_ref[...]).astype(BF16)
    merged = (jax.nn.sigmoid(_dot(h, wga_ref[...])) * _dot(oa_ref[...], wua_ref[...])
              + jax.nn.sigmoid(_dot(h, wgb_ref[...])) * _dot(ob_ref[...], wub_ref[...]))
    o_ref[...] = x + _dot(merged.astype(BF16), wo_ref[...])


def _merge(x2d, g, oa, ob, wga, wgb, wua, wub, wo):
    n, d = x2d.shape
    tm = min(ROW_TILE, n)
    full = lambda w: pl.BlockSpec(w.shape, lambda i: (0, 0))
    row = lambda width: pl.BlockSpec((tm, width), lambda i: (i, 0))
    return pl.pallas_call(
        _merge_kernel,
        grid=(n // tm,),
        in_specs=[row(d), full(g), row(oa.shape[1]), row(ob.shape[1]), full(wga), full(wgb),
                  full(wua), full(wub), full(wo)],
        out_specs=row(d),
        out_shape=jax.ShapeDtypeStruct((n, d), F32),
        compiler_params=_params("parallel"),
        name="merge",
    )(x2d, g, oa, ob, wga, wgb, wua, wub, wo)


def _memkv_kernel(m_ref, g_ref, w_ref, o_ref):
    o_ref[...] = _dot(_rms(m_ref[...], g_ref[...]).astype(BF16), w_ref[...]).astype(BF16)


def _memkv(m2d, g, w):
    n, d = m2d.shape
    tm = min(ROW_TILE, n)
    return pl.pallas_call(
        _memkv_kernel,
        grid=(n // tm,),
        in_specs=[pl.BlockSpec((tm, d), lambda i: (i, 0)), pl.BlockSpec(g.shape, lambda i: (0, 0)),
                  pl.BlockSpec(w.shape, lambda i: (0, 0))],
        out_specs=pl.BlockSpec((tm, w.shape[1]), lambda i: (i, 0)),
        out_shape=jax.ShapeDtypeStruct((n, w.shape[1]), BF16),
        compiler_params=_params("parallel"),
        name="memkv",
    )(m2d, g, w)


def _xattn_kernel(x_ref, g_ref, kv_ref, wq_ref, wo_ref, o_ref):
    x = x_ref[0]
    d = x.shape[1]
    hd = d // XATTN_HEADS
    q = _dot(_rms(x, g_ref[...]).astype(BF16), wq_ref[...]).astype(BF16)
    outs = []
    for h in range(XATTN_HEADS):
        k = kv_ref[0, :, h * hd:(h + 1) * hd]
        v = kv_ref[0, :, d + h * hd:d + (h + 1) * hd]
        s = _dot_nt(q[:, h * hd:(h + 1) * hd], k)
        e = jnp.exp2(s - jnp.max(s, axis=-1, keepdims=True))
        outs.append((_dot(e.astype(BF16), v) / jnp.sum(e, axis=-1, keepdims=True)).astype(BF16))
    o_ref[0] = x + _dot(jnp.concatenate(outs, axis=1), wo_ref[...])


def _xattn(x3d, g, kv, wq, wo):
    b, s, d = x3d.shape
    tm = min(XATTN_ROWS, s)
    full = lambda w: pl.BlockSpec(w.shape, lambda i, j: (0, 0))
    return pl.pallas_call(
        _xattn_kernel,
        grid=(b, s // tm),
        in_specs=[pl.BlockSpec((1, tm, d), lambda i, j: (i, j, 0)), full(g),
                  pl.BlockSpec((1,) + kv.shape[1:], lambda i, j: (i, 0, 0)), full(wq), full(wo)],
        out_specs=pl.BlockSpec((1, tm, d), lambda i, j: (i, j, 0)),
        out_shape=jax.ShapeDtypeStruct((b, s, d), F32),
        compiler_params=_params("parallel", "parallel"),
        name="xattn",
    )(x3d, g, kv, wq, wo)


def _rel_bucket(dist):
    dist = jnp.maximum(dist, 0)
    max_exact = REL_BUCKETS // 2
    d = jnp.maximum(dist, 1).astype(F32)
    large = max_exact + (jnp.log(d / max_exact) / math.log(REL_MAX_DIST / max_exact)
                         * (REL_BUCKETS - max_exact)).astype(jnp.int32)
    large = jnp.minimum(large, REL_BUCKETS - 1)
    return jnp.where(dist < max_exact, dist, large)


def _bias_of_dist(bias, dist):
    bucket = _rel_bucket(dist)[None]
    out = jnp.zeros((bias.shape[1],) + dist.shape, F32)
    for k in range(REL_BUCKETS):
        out = jnp.where(bucket == k, bias[k].reshape((-1,) + (1,) * dist.ndim), out)
    return out


def _bias_tile(bias, shift, window_chunks, hi):
    win = window_chunks * TQ
    dist = jnp.arange(TQ)[:, None] + (window_chunks - 1) * TQ - jnp.arange(win)[None, :]
    vals = _bias_of_dist(bias, dist) - shift[:, None, None]
    return jnp.where((dist >= 0) & (dist < hi), vals, NEG).reshape(-1, win)


def _cmp_tables(w1, pe):
    half = CMP_BLOCK // 2
    eye = jnp.eye(NSA_GROUPS, dtype=F32)
    w = w1.reshape(2, half, HEAD_DIM, CMP_HIDDEN)
    wexp = jnp.einsum('aldh,gk->algdkh', w, eye)
    wexp = wexp.reshape(2, half * NSA_GROUPS * HEAD_DIM, NSA_GROUPS * CMP_HIDDEN)
    pexp = jnp.broadcast_to(pe.reshape(2, half, 1, HEAD_DIM), (2, half, NSA_GROUPS, HEAD_DIM))
    return wexp[0].astype(BF16), wexp[1].astype(BF16), pexp.reshape(2, -1)


def _block_diag2(w):
    z = jnp.zeros_like(w)
    return jnp.concatenate([jnp.concatenate([w, z], axis=1), jnp.concatenate([z, w], axis=1)], axis=0)


def kernel(x, mem, norm_ffn1, w1_gate, w1_up, w1_down, norm_mix, w_in, cmp_pe_k, cmp_w1_k, cmp_w2_k,
           cmp_pe_v, cmp_w1_v, cmp_w2_v, attn_sinks, rel_bias, w_up_a, w_up_b, w_out, norm_xattn,
           norm_mem, w_xq, w_xkv, w_xo, norm_ffn2, w2_gate, w2_up, w2_down, norm_final):
    b, s, d = x.shape
    n = b * s
    assert norm_ffn1.shape[0] == 1, "single-layer kernel"
    assert s % FAR_CHUNK == 0 and s >= 2 * NSA_WINDOW
    n_cmp = (s - CMP_BLOCK) // CMP_STRIDE + 1
    ncp = s // CMP_STRIDE
    n_sel = s // SEL_BLOCK
    assert n_sel <= LANES, "selection blocks must fit the 128 mask lanes"
    bf = lambda w: w.astype(BF16)
    scale = HEAD_DIM ** -0.5
    log2e = math.log2(math.e)

    wi = w_in[0]
    nq = NSA_HEADS * HEAD_DIM
    nkv = NSA_GROUPS * HEAD_DIM
    o_g = nq + 6 * nkv
    o_qb = o_g + 3 * NSA_HEADS
    o_kb = o_qb + SWA_HEADS * HEAD_DIM
    o_ga = o_kb + 2 * SWA_KV_HEADS * HEAD_DIM
    wqa = bf(wi[:, :nq] * (scale * log2e))
    wqb = bf(wi[:, o_qb:o_kb] * (scale * log2e))
    wkv = bf(jnp.concatenate([wi[:, nq + 2 * nkv:o_g], wi[:, o_kb:o_ga]], axis=1))
    wc = bf(wi[:, nq:nq + 2 * nkv])
    wgn = bf(jnp.pad(wi[:, o_g:o_qb], ((0, 0), (0, LANES - 3 * NSA_HEADS))))
    wga = bf(wi[:, o_ga:o_ga + d])
    wgb = bf(wi[:, o_ga + d:o_ga + 2 * d])
    wka, wkb, pek = _cmp_tables(cmp_w1_k[0], cmp_pe_k[0])
    wva, wvb, pev = _cmp_tables(cmp_w1_v[0], cmp_pe_v[0])
    w2k = bf(_block_diag2(cmp_w2_k[0]))
    w2v = bf(_block_diag2(cmp_w2_v[0]))

    bias_a = rel_bias[:, :NSA_HEADS] * log2e
    bias_b = rel_bias[:, NSA_HEADS:] * log2e
    far_bias = _bias_of_dist(bias_a, jnp.full((1,), REL_MAX_DIST, jnp.int32))[:, 0]
    no_shift = jnp.zeros((SWA_HEADS,), F32)
    tw = _bias_tile(bias_a, far_bias, NSA_WIN_CHUNKS, NSA_WINDOW)
    ts = _bias_tile(bias_a, far_bias, NSA_WIN_CHUNKS, NSA_WIN_CHUNKS * TQ)
    tb = _bias_tile(bias_b, no_shift, SWA_WIN_CHUNKS, SWA_WINDOW)
    dist_c = (jnp.arange(TQ)[:, None] + (CMP_WIN_BACK * CMP_STRIDE - (CMP_BLOCK - 1))
              - CMP_STRIDE * jnp.arange(CMP_BLOCK)[None, :])
    dcv = jnp.where(dist_c < REL_MAX_DIST, _bias_of_dist(bias_a, dist_c) - far_bias[:, None, None], 0.0)
    dcv = jnp.where(dist_c >= 0, dcv, NEG).reshape(NSA_HEADS * TQ, CMP_BLOCK)
    dc_hi = dcv.astype(BF16)
    dc_lo = (dcv - dc_hi.astype(F32)).astype(BF16)
    after = jnp.full((NSA_HEADS * TQ, 1), NEG, BF16)
    ca = jnp.concatenate([dc_hi, dc_lo, after,
                          jnp.zeros((NSA_HEADS * TQ, LANES - 2 * CMP_BLOCK - 1), BF16)], axis=1)
    sink = jnp.broadcast_to(jnp.repeat(attn_sinks[0] * log2e, TQ)[:, None], (SWA_HEADS * TQ, LANES))

    cs = np.arange(ncp)[:, None] * CMP_STRIDE
    ss = np.arange(LANES)[None, :] * SEL_BLOCK
    ovl = np.maximum(np.minimum(cs + CMP_BLOCK, ss + SEL_BLOCK) - np.maximum(cs, ss), 0) / CMP_BLOCK
    ovl[n_cmp:] = 0.0
    ov = jnp.asarray(ovl, dtype=BF16)

    row = lambda g: g.reshape(1, -1)
    x1 = _ffn(x.reshape(n, d), row(norm_ffn1[0]), bf(w1_gate[0]), bf(w1_up[0]), bf(w1_down[0]),
              row(norm_final), final_norm=False)
    qa, qb, kv, kcf, vcf, gn = _inproj(x1, row(norm_mix[0]), wqa, wqb, wkv, wc, wgn, seq_len=s)
    kc, vc = _compress(kcf.reshape(b, s, LANES), vcf.reshape(b, s, LANES), pek, pev,
                       wka, wkb, wva, wvb, w2k, w2v, ov)
    oa, ob = _mix(qa.reshape(b, s, -1), qb.reshape(b, s, -1), gn.reshape(b, s, LANES), kc, vc,
                  kv.reshape(b, s, -1), ca, tw[:, :TQ], tw[:, -2 * TQ:], ts[:, -2 * TQ:], tb, sink,
                  n_sel=n_sel)
    x2 = _merge(x1, row(norm_mix[0]), oa.reshape(n, -1), ob.reshape(n, -1), wga, wgb,
                bf(w_up_a[0]), bf(w_up_b[0]), bf(w_out[0]))
    mkv = _memkv(mem.reshape(-1, d), row(norm_mem[0]), bf(w_xkv[0]))
    x3 = _xattn(x2.reshape(b, s, d), row(norm_xattn[0]), mkv.reshape(b, -1, 2 * d),
                bf(w_xq[0] * ((d // XATTN_HEADS) ** -0.5 * log2e)), bf(w_xo[0]))
    out = _ffn(x3.reshape(n, d), row(norm_ffn2[0]), bf(w2_gate[0]), bf(w2_up[0]), bf(w2_down[0]),
               row(norm_final), final_norm=True)
    return out.reshape(b, s, d)
```

```python
import functools
import math

import numpy as np
import jax
import jax.numpy as jnp
from jax import lax
from jax.experimental import pallas as pl
from jax.experimental.pallas import tpu as pltpu

F32 = jnp.float32
BF16 = jnp.bfloat16

HEAD_DIM = 64
NSA_HEADS = 8
NSA_GROUPS = 2
HEADS_PER_GROUP = NSA_HEADS // NSA_GROUPS
CMP_BLOCK = 32
CMP_STRIDE = 16
CMP_HIDDEN = 256
SEL_BLOCK = 64
SEL_TOPN = 16
NSA_WINDOW = 512
SWA_HEADS = 8
SWA_KV_HEADS = 2
SWA_WINDOW = 128
REL_BUCKETS = 32
REL_MAX_DIST = 128
XATTN_HEADS = 4
EPS = 1e-6
NEG = -1e30

LANES = 128
MXU_WIDTH = 256
FFN_ROWS = 1024
ROW_TILE = 512
XATTN_ROWS = 1024
TQ = 128
FAR_CHUNK = 512
NSA_WIN_CHUNKS = NSA_WINDOW // TQ + 1
SWA_WIN_CHUNKS = SWA_WINDOW // TQ + 1
CMP_WIN_BACK = 16
VMEM_LIMIT = 56 * 1024 * 1024


def _rms(x, g):
    return x * lax.rsqrt(jnp.mean(x * x, axis=-1, keepdims=True) + EPS) * g


def _dot(a, b):
    return jnp.dot(a, b, preferred_element_type=F32)


def _dot_nt(a, b):
    return lax.dot_general(a, b, (((1,), (1,)), ((), ())), preferred_element_type=F32)


def _params(*sem):
    return pltpu.CompilerParams(dimension_semantics=sem, vmem_limit_bytes=VMEM_LIMIT)


def _ffn_kernel(x_ref, g_ref, wg_ref, wu_ref, wd_ref, gf_ref, o_ref, h_scr, acc_scr, *, final_norm):
    def down(h, j):
        a = _dot(h, wg_ref[j])
        u = _dot(h, wu_ref[j])
        return _dot((a * jax.nn.sigmoid(a) * u).astype(BF16), wd_ref[j])

    h0 = _rms(x_ref[...], g_ref[...]).astype(BF16)
    h_scr[...] = h0
    acc_scr[...] = down(h0, 0)

    def chunk(j, carry):
        acc_scr[...] += down(h_scr[...], j)
        return carry

    last = wg_ref.shape[0] - 1
    lax.fori_loop(1, last, chunk, 0)
    y = x_ref[...] + 0.5 * (acc_scr[...] + down(h_scr[...], last))
    if final_norm:
        y = _rms(y, gf_ref[...])
    o_ref[...] = y


def _ffn(x2d, g, wg, wu, wd, gf, *, final_norm):
    n, d = x2d.shape
    dff = wg.shape[1]
    tm = min(FFN_ROWS, n)
    tf = MXU_WIDTH
    nc = dff // tf
    assert nc >= 2, "first and last chunk are peeled"
    once = pl.Buffered(1)
    chunks = lambda shape: pl.BlockSpec(shape, lambda i: (0, 0, 0), pipeline_mode=once)
    return pl.pallas_call(
        functools.partial(_ffn_kernel, final_norm=final_norm),
        grid=(n // tm,),
        in_specs=[
            pl.BlockSpec((tm, d), lambda i: (i, 0)),
            pl.BlockSpec((1, d), lambda i: (0, 0)),
            chunks((nc, d, tf)), chunks((nc, d, tf)), chunks((nc, tf, d)),
            pl.BlockSpec((1, d), lambda i: (0, 0)),
        ],
        out_specs=pl.BlockSpec((tm, d), lambda i: (i, 0)),
        out_shape=jax.ShapeDtypeStruct((n, d), F32),
        scratch_shapes=[pltpu.VMEM((tm, d), BF16), pltpu.VMEM((tm, d), F32)],
        compiler_params=_params("parallel"),
        name="ffn",
    )(x2d, g, wg.reshape(d, nc, tf).transpose(1, 0, 2), wu.reshape(d, nc, tf).transpose(1, 0, 2),
      wd.reshape(nc, tf, d), gf)


def _expand_head_pairs(r, heads, kv_heads):
    lane = lax.broadcasted_iota(jnp.int32, (r.shape[0], LANES), 1)
    low = lane < HEAD_DIM
    tiles = []
    for p in range(heads // 2):
        pair = r[:, p * LANES:(p + 1) * LANES]
        swapped = pltpu.roll(pair, HEAD_DIM, 1)
        if (2 * p) // (heads // kv_heads) == 0:
            tiles += [jnp.where(low, pair, 0.0), jnp.where(low, swapped, 0.0)]
        else:
            tiles += [jnp.where(low, 0.0, swapped), jnp.where(low, 0.0, pair)]
    return tiles


def _compact_head_pairs(tiles, heads, kv_heads):
    lane = lax.broadcasted_iota(jnp.int32, tiles[0].shape, 1)
    low = lane < HEAD_DIM
    out = []
    for p in range(heads // 2):
        a, b = tiles[2 * p], tiles[2 * p + 1]
        if (2 * p) // (heads // kv_heads) == 0:
            out.append(jnp.where(low, a, pltpu.roll(b, HEAD_DIM, 1)))
        else:
            out.append(jnp.where(low, pltpu.roll(a, HEAD_DIM, 1), b))
    return jnp.concatenate(out, axis=1)


def _inproj_kernel(x_ref, g_ref, wqa_ref, wqb_ref, wkv_ref, wc_ref, wg_ref,
                   qa_ref, qb_ref, kv_ref, kc_ref, vc_ref, gn_ref, *, seq_len):
    tm = x_ref.shape[0]
    h = _rms(x_ref[...], g_ref[...]).astype(BF16)
    for w_ref, q_ref, heads, kv_heads in ((wqa_ref, qa_ref, NSA_HEADS, NSA_GROUPS),
                                          (wqb_ref, qb_ref, SWA_HEADS, SWA_KV_HEADS)):
        tiles = _expand_head_pairs(_dot(h, w_ref[...]), heads, kv_heads)
        for hh, t in enumerate(tiles):
            q_ref[:, hh * LANES:(hh + 1) * LANES] = t.astype(BF16)
    kv = _dot(h, wkv_ref[...]).astype(BF16)
    pos = (pl.program_id(0) * tm + lax.broadcasted_iota(jnp.int32, (tm, LANES), 0)) % seq_len
    lane = lax.broadcasted_iota(jnp.int32, (tm, LANES), 1)
    block_onehot = jnp.where(lane == pos // SEL_BLOCK, 1.0, 0.0).astype(BF16)
    ones = jnp.ones((tm, LANES), BF16)
    for i in range(kv.shape[1] // LANES):
        kv_ref[:, 2 * i * LANES:(2 * i + 1) * LANES] = kv[:, i * LANES:(i + 1) * LANES]
        kv_ref[:, (2 * i + 1) * LANES:(2 * i + 2) * LANES] = block_onehot if i % 2 == 0 else ones
    c = _dot(h, wc_ref[...])
    kc_ref[...] = c[:, :LANES]
    vc_ref[...] = c[:, LANES:]
    gn_ref[...] = jax.nn.sigmoid(_dot(h, wg_ref[...]))


def _inproj(x2d, g, wqa, wqb, wkv, wc, wg, *, seq_len):
    n, d = x2d.shape
    kv_w = 2 * wkv.shape[1]
    tm = min(ROW_TILE, n)
    full = lambda w: pl.BlockSpec(w.shape, lambda i: (0, 0))
    row = lambda width: pl.BlockSpec((tm, width), lambda i: (i, 0))
    return pl.pallas_call(
        functools.partial(_inproj_kernel, seq_len=seq_len),
        grid=(n // tm,),
        in_specs=[row(d), full(g), full(wqa), full(wqb), full(wkv), full(wc), full(wg)],
        out_specs=[row(2 * wqa.shape[1]), row(2 * wqb.shape[1]), row(kv_w), row(LANES), row(LANES),
                   row(LANES)],
        out_shape=[
            jax.ShapeDtypeStruct((n, 2 * wqa.shape[1]), BF16),
            jax.ShapeDtypeStruct((n, 2 * wqb.shape[1]), BF16),
            jax.ShapeDtypeStruct((n, kv_w), BF16),
            jax.ShapeDtypeStruct((n, LANES), F32),
            jax.ShapeDtypeStruct((n, LANES), F32),
            jax.ShapeDtypeStruct((n, LANES), F32),
        ],
        compiler_params=_params("parallel"),
        name="inproj",
    )(x2d, g, wqa, wqb, wkv, wc, wg)


def _compress_kernel(ak_ref, av_ref, pek_ref, pev_ref, wka_ref, wkb_ref, wva_ref, wvb_ref,
                     w2k_ref, w2v_ref, ov_ref, kc_ref, vc_ref):
    rows = kc_ref.shape[1]

    def one(a_ref, pe_ref, wa_ref, wb_ref, w2_ref):
        u = v = None
        for l in range(CMP_STRIDE):
            tok = a_ref[0, pl.ds(l, rows, stride=CMP_STRIDE), :]
            lanes = slice(l * LANES, (l + 1) * LANES)
            du = _dot((tok + pe_ref[0:1, lanes]).astype(BF16), wa_ref[lanes, :])
            dv = _dot((tok + pe_ref[1:2, lanes]).astype(BF16), wb_ref[lanes, :])
            u = du if u is None else u + du
            v = dv if v is None else v + dv
        hid = u + pltpu.roll(v, rows - 1, 0)
        hid = jax.nn.gelu(hid, approximate=True)
        return _dot(hid.astype(BF16), w2_ref[...]).astype(BF16)

    kc_ref[0] = one(ak_ref, pek_ref, wka_ref, wkb_ref, w2k_ref)
    vc_ref[0, :, :LANES] = one(av_ref, pev_ref, wva_ref, wvb_ref, w2v_ref)
    vc_ref[0, :, LANES:2 * LANES] = ov_ref[...]
    vc_ref[0, :, 2 * LANES:] = jnp.ones(ov_ref.shape, BF16)


def _compress(ak, av, pek, pev, wka, wkb, wva, wvb, w2k, w2v, ov):
    b, s, width = ak.shape
    ncp = s // CMP_STRIDE
    full = lambda w: pl.BlockSpec(w.shape, lambda i: (0,) * w.ndim)
    per_b = lambda w: pl.BlockSpec((1, ncp, w), lambda i: (i, 0, 0))
    tokens = pl.BlockSpec((1, s, width), lambda i: (i, 0, 0))
    return pl.pallas_call(
        _compress_kernel,
        grid=(b,),
        in_specs=[tokens, tokens, full(pek), full(pev), full(wka), full(wkb),
                  full(wva), full(wvb), full(w2k), full(w2v), full(ov)],
        out_specs=[per_b(LANES), per_b(3 * LANES)],
        out_shape=[jax.ShapeDtypeStruct((b, ncp, LANES), BF16),
                   jax.ShapeDtypeStruct((b, ncp, 3 * LANES), BF16)],
        compiler_params=_params("parallel"),
        name="compress",
    )(ak, av, pek, pev, wka, wkb, wva, wvb, w2k, w2v, ov)


def _stack_heads(q, heads):
    return jnp.concatenate([q[:, h * LANES:(h + 1) * LANES] for h in range(heads)], axis=0)


def _window_rows(ref, s0, window_chunks, fallback):
    parts = []
    for r in range(window_chunks):
        start = s0 + (r - (window_chunks - 1)) * TQ
        start = jnp.where(start < 0, fallback, start)
        parts.append(ref[0, pl.ds(pl.multiple_of(start, TQ), TQ), :])
    return jnp.concatenate(parts, axis=0)


def _lanes(a, b):
    return jnp.concatenate([a, b], axis=1)


def _per_head(tiles_per_group):
    return jnp.concatenate([tiles_per_group[h // HEADS_PER_GROUP] for h in range(NSA_HEADS)], axis=0)


def _mix_kernel(qa_ref, qb_ref, gn_ref, kc_ref, vc_ref, ks_ref, vs_ref, kw_ref, vw_ref, kb_ref, vb_ref,
                ca_ref, tw0_ref, tw1_ref, ts_ref, tb_ref, sink_ref,
                oa_ref, ob_ref, m_scr, acc_scr, sfa_scr, sfb_scr, *, n_sel):
    c = pl.program_id(1)
    s0 = c * TQ
    rows = NSA_HEADS * TQ
    ncp = kc_ref.shape[1]
    nsp = LANES
    win = NSA_WIN_CHUNKS * TQ

    q = _stack_heads(qa_ref[0], NSA_HEADS)
    t_col = s0 + lax.broadcasted_iota(jnp.int32, (rows, LANES), 0) % TQ

    n_i = lax.broadcasted_iota(jnp.int32, (ncp, LANES), 0)
    lane_c = lax.broadcasted_iota(jnp.int32, (ncp, LANES), 1)
    w0 = s0 // CMP_STRIDE - CMP_WIN_BACK
    slot = jnp.where(lane_c < CMP_BLOCK, lane_c, lane_c - CMP_BLOCK)
    place = ((lane_c < 2 * CMP_BLOCK) & (n_i == w0 + slot)) | (
        (lane_c == 2 * CMP_BLOCK) & (n_i >= w0 + CMP_BLOCK))
    kc_aug = _lanes(kc_ref[0], jnp.where(place, 1.0, 0.0).astype(BF16))
    s_c = _dot_nt(_lanes(q, ca_ref[...]), kc_aug)
    e_c = jnp.exp2(s_c - jnp.max(s_c, axis=-1, keepdims=True)).astype(BF16)
    r_c = _dot(e_c, vc_ref[0])
    inv_c = jnp.where(t_col >= CMP_BLOCK - 1, 1.0 / r_c[:, 2 * LANES:], 0.0)
    o_c = r_c[:, :LANES] * inv_c
    imp_h = r_c[:, LANES:2 * LANES] * inv_c

    t_q = s0 + lax.broadcasted_iota(jnp.int32, (TQ, 1), 0)
    cur_q = t_q // SEL_BLOCK
    blk_l = lax.broadcasted_iota(jnp.int32, (TQ, nsp), 1)
    forced = (blk_l == 0) | (blk_l == cur_q) | (blk_l == cur_q - 1)
    causal_blk = blk_l * SEL_BLOCK <= t_q
    cand = causal_blk & jnp.logical_not(forced)
    n_sweeps = min(SEL_TOPN, n_sel) - 3
    blk_s = lax.broadcasted_iota(jnp.int32, (nsp, TQ), 0).astype(F32)
    first_blk = (s0 - NSA_WINDOW) // SEL_BLOCK

    near_q, far_q = [], []
    for g in range(NSA_GROUPS):
        base = g * HEADS_PER_GROUP * TQ
        imp = imp_h[base:base + TQ]
        for h in range(1, HEADS_PER_GROUP):
            imp = imp + imp_h[base + h * TQ:base + (h + 1) * TQ]
        v_t = jnp.where(cand, imp, -1.0).T
        for _ in range(n_sweeps):
            mx = jnp.max(v_t, axis=0, keepdims=True)
            first = jnp.min(jnp.where(v_t == mx, blk_s, float(nsp)), axis=0, keepdims=True)
            v_t = jnp.where(blk_s == first, -2.0, v_t)
        chosen = (forced | (v_t.T == -2.0)) & causal_blk
        near_q.append(jnp.where(chosen, 0.0, NEG).astype(BF16))
        far_q.append(jnp.where(chosen & (blk_l < first_blk), 0.0, NEG).astype(BF16))

    s_len = ks_ref.shape[1]
    spare = s_len - TQ
    causal_q = _per_head([jnp.where(causal_blk, 0.0, NEG).astype(BF16)] * NSA_GROUPS)

    def add_tiles(s, first_ref, last_ref):
        first = s[:, :TQ] if first_ref is None else s[:, :TQ] + first_ref[...]
        return jnp.concatenate([first, s[:, TQ:win - 2 * TQ], s[:, win - 2 * TQ:] + last_ref[...]], axis=1)

    s_w = _dot_nt(_lanes(q, causal_q), _window_rows(kw_ref, s0, NSA_WIN_CHUNKS, spare))
    s_w = add_tiles(s_w, tw0_ref, tw1_ref)
    e_w = jnp.exp2((s_w - jnp.max(s_w, axis=-1, keepdims=True)).astype(BF16))
    r_w = _dot(e_w, _window_rows(vw_ref, s0, NSA_WIN_CHUNKS, spare))
    o_w = r_w[:, :LANES] / r_w[:, LANES:]

    s_n = _dot_nt(_lanes(q, _per_head(near_q)), _window_rows(ks_ref, s0, NSA_WIN_CHUNKS, spare))
    s_n = add_tiles(s_n, None, ts_ref)
    m_n = jnp.broadcast_to(jnp.max(s_n, axis=-1, keepdims=True), m_scr.shape)
    m_scr[...] = m_n
    acc_scr[...] = _dot(jnp.exp2((s_n - jnp.concatenate([m_n] * NSA_WIN_CHUNKS, axis=1)).astype(BF16)),
                        _window_rows(vs_ref, s0, NSA_WIN_CHUNKS, spare))

    q_far = _lanes(q, _per_head(far_q))
    n_far = jnp.maximum(s0 - NSA_WINDOW + FAR_CHUNK - 1, 0) // FAR_CHUNK
    last_chunk = s_len // FAR_CHUNK - 1
    group_rows = [slice(g * rows // NSA_GROUPS, (g + 1) * rows // NSA_GROUPS) for g in range(NSA_GROUPS)]

    def far_rows(j):
        return pl.ds(pl.multiple_of(jnp.minimum(j, last_chunk) * FAR_CHUNK, FAR_CHUNK), FAR_CHUNK)

    def far_logits(j, buf):
        for r in group_rows:
            buf[r, :] = _dot_nt(q_far[r], ks_ref[0, far_rows(j), :])

    def far_update(buf, j):
        for r in group_rows:
            s_f = buf[r, :]
            m_old = m_scr[r, :]
            m_new = jnp.maximum(m_old, jnp.broadcast_to(jnp.max(s_f, axis=-1, keepdims=True), m_old.shape))
            e_f = jnp.exp2((s_f - jnp.concatenate([m_new] * (FAR_CHUNK // LANES), axis=1)).astype(BF16))
            alpha = jnp.exp2(m_old - m_new)
            acc_scr[r, :] = _lanes(alpha, alpha) * acc_scr[r, :] + _dot(e_f, vs_ref[0, far_rows(j), :])
            m_scr[r, :] = m_new

    far_logits(0, sfa_scr)

    def far_step(i, carry):
        far_logits(2 * i + 1, sfb_scr)
        far_update(sfa_scr, 2 * i)
        far_logits(2 * i + 2, sfa_scr)
        far_update(sfb_scr, 2 * i + 1)
        return carry

    lax.fori_loop(0, n_far // 2, far_step, 0)

    @pl.when(n_far % 2 == 1)
    def _():
        far_update(sfa_scr, n_far - 1)

    o_s = acc_scr[:, :LANES] / acc_scr[:, LANES:]

    gn = gn_ref[0]
    tiles = []
    for h in range(NSA_HEADS):
        r = slice(h * TQ, (h + 1) * TQ)
        tiles.append(gn[:, h:h + 1] * o_c[r] + gn[:, NSA_HEADS + h:NSA_HEADS + h + 1] * o_s[r]
                     + gn[:, 2 * NSA_HEADS + h:2 * NSA_HEADS + h + 1] * o_w[r])
    oa_ref[0] = _compact_head_pairs(tiles, NSA_HEADS, NSA_GROUPS).astype(BF16)

    s_b = _dot_nt(_lanes(_stack_heads(qb_ref[0], SWA_HEADS), causal_q),
                  _window_rows(kb_ref, s0, SWA_WIN_CHUNKS, spare)) + tb_ref[...]
    sink = sink_ref[...]
    m_b = jnp.maximum(jnp.broadcast_to(jnp.max(s_b, axis=-1, keepdims=True), sink.shape), sink)
    e_b = jnp.exp2((s_b - jnp.concatenate([m_b] * SWA_WIN_CHUNKS, axis=1)).astype(BF16))
    r_b = _dot(e_b, _window_rows(vb_ref, s0, SWA_WIN_CHUNKS, spare))
    o_b = r_b[:, :LANES] / (r_b[:, LANES:] + jnp.exp2(sink - m_b))
    ob_ref[0] = _compact_head_pairs([o_b[h * TQ:(h + 1) * TQ] for h in range(SWA_HEADS)],
                                    SWA_HEADS, SWA_KV_HEADS).astype(BF16)


def _mix(qa, qb, gn, kc, vc, kv, ca, tw0, tw1, ts, tb, sink, *, n_sel):
    b, s, qw = qa.shape
    ncp = kc.shape[1]
    rows = NSA_HEADS * TQ
    once = pl.Buffered(1)
    full = lambda w: pl.BlockSpec(w.shape, lambda i, j: (0,) * w.ndim, pipeline_mode=once)
    per_b = lambda blk, col, w: pl.BlockSpec((1, blk, w), lambda i, j: (i, 0, col), pipeline_mode=once)
    tile = lambda w: pl.BlockSpec((1, TQ, w), lambda i, j: (i, j, 0))
    return pl.pallas_call(
        functools.partial(_mix_kernel, n_sel=n_sel),
        grid=(b, s // TQ),
        in_specs=[tile(qw), tile(qw), tile(LANES), per_b(ncp, 0, LANES), per_b(ncp, 0, 3 * LANES)]
        + [per_b(s, col, 2 * LANES) for col in range(6)]
        + [full(ca), full(tw0), full(tw1), full(ts), full(tb), full(sink)],
        out_specs=[tile(qw // 2), tile(qw // 2)],
        out_shape=[jax.ShapeDtypeStruct((b, s, qw // 2), BF16)] * 2,
        scratch_shapes=[pltpu.VMEM((rows, LANES), F32), pltpu.VMEM((rows, 2 * LANES), F32),
                        pltpu.VMEM((rows, FAR_CHUNK), F32), pltpu.VMEM((rows, FAR_CHUNK), F32)],
        compiler_params=_params("parallel", "arbitrary"),
        name="mix",
    )(qa, qb, gn, kc, vc, kv, kv, kv, kv, kv, kv, ca, tw0, tw1, ts, tb, sink)


def _merge_kernel(x_ref, g_ref, oa_ref, ob_ref, wga_ref, wgb_ref, wua_ref, wub_ref, wo_ref, o_ref):
    x = x_ref[...]
    h = _rms(x, g_ref[...]).astype(BF16)
    merged = (jax.nn.sigmoid(_dot(h, wga_ref[...])) * _dot(oa_ref[...], wua_ref[...])
              + jax.nn.sigmoid(_dot(h, wgb_ref[...])) * _dot(ob_ref[...], wub_ref[...]))
    o_ref[...] = x + _dot(merged.astype(BF16), wo_ref[...])


def _merge(x2d, g, oa, ob, wga, wgb, wua, wub, wo):
    n, d = x2d.shape
    tm = min(ROW_TILE, n)
    full = lambda w: pl.BlockSpec(w.shape, lambda i: (0, 0))
    row = lambda width: pl.BlockSpec((tm, width), lambda i: (i, 0))
    return pl.pallas_call(
        _merge_kernel,
        grid=(n // tm,),
        in_specs=[row(d), full(g), row(oa.shape[1]), row(ob.shape[1]), full(wga), full(wgb),
                  full(wua), full(wub), full(wo)],
        out_specs=row(d),
        out_shape=jax.ShapeDtypeStruct((n, d), F32),
        compiler_params=_params("parallel"),
        name="merge",
    )(x2d, g, oa, ob, wga, wgb, wua, wub, wo)


def _memkv_kernel(m_ref, g_ref, w_ref, o_ref):
    o_ref[...] = _dot(_rms(m_ref[...], g_ref[...]).astype(BF16), w_ref[...]).astype(BF16)


def _memkv(m2d, g, w):
    n, d = m2d.shape
    tm = min(ROW_TILE, n)
    return pl.pallas_call(
        _memkv_kernel,
        grid=(n // tm,),
        in_specs=[pl.BlockSpec((tm, d), lambda i: (i, 0)), pl.BlockSpec(g.shape, lambda i: (0, 0)),
                  pl.BlockSpec(w.shape, lambda i: (0, 0))],
        out_specs=pl.BlockSpec((tm, w.shape[1]), lambda i: (i, 0)),
        out_shape=jax.ShapeDtypeStruct((n, w.shape[1]), BF16),
        compiler_params=_params("parallel"),
        name="memkv",
    )(m2d, g, w)


def _xattn_kernel(x_ref, g_ref, kv_ref, wq_ref, wo_ref, o_ref):
    x = x_ref[0]
    d = x.shape[1]
    hd = d // XATTN_HEADS
    q = _dot(_rms(x, g_ref[...]).astype(BF16), wq_ref[...]).astype(BF16)
    outs = []
    for h in range(XATTN_HEADS):
        k = kv_ref[0, :, h * hd:(h + 1) * hd]
        v = kv_ref[0, :, d + h * hd:d + (h + 1) * hd]
        s = _dot_nt(q[:, h * hd:(h + 1) * hd], k)
        e = jnp.exp2(s - jnp.max(s, axis=-1, keepdims=True))
        outs.append((_dot(e.astype(BF16), v) / jnp.sum(e, axis=-1, keepdims=True)).astype(BF16))
    o_ref[0] = x + _dot(jnp.concatenate(outs, axis=1), wo_ref[...])


def _xattn(x3d, g, kv, wq, wo):
    b, s, d = x3d.shape
    tm = min(XATTN_ROWS, s)
    full = lambda w: pl.BlockSpec(w.shape, lambda i, j: (0, 0))
    return pl.pallas_call(
        _xattn_kernel,
        grid=(b, s // tm),
        in_specs=[pl.BlockSpec((1, tm, d), lambda i, j: (i, j, 0)), full(g),
                  pl.BlockSpec((1,) + kv.shape[1:], lambda i, j: (i, 0, 0)), full(wq), full(wo)],
        out_specs=pl.BlockSpec((1, tm, d), lambda i, j: (i, j, 0)),
        out_shape=jax.ShapeDtypeStruct((b, s, d), F32),
        compiler_params=_params("parallel", "parallel"),
        name="xattn",
    )(x3d, g, kv, wq, wo)


def _rel_bucket(dist):
    dist = jnp.maximum(dist, 0)
    max_exact = REL_BUCKETS // 2
    d = jnp.maximum(dist, 1).astype(F32)
    large = max_exact + (jnp.log(d / max_exact) / math.log(REL_MAX_DIST / max_exact)
                         * (REL_BUCKETS - max_exact)).astype(jnp.int32)
    large = jnp.minimum(large, REL_BUCKETS - 1)
    return jnp.where(dist < max_exact, dist, large)


def _bias_of_dist(bias, dist):
    bucket = _rel_bucket(dist)[None]
    out = jnp.zeros((bias.shape[1],) + dist.shape, F32)
    for k in range(REL_BUCKETS):
        out = jnp.where(bucket == k, bias[k].reshape((-1,) + (1,) * dist.ndim), out)
    return out


def _bias_tile(bias, shift, window_chunks, hi):
    win = window_chunks * TQ
    dist = jnp.arange(TQ)[:, None] + (window_chunks - 1) * TQ - jnp.arange(win)[None, :]
    vals = _bias_of_dist(bias, dist) - shift[:, None, None]
    return jnp.where((dist >= 0) & (dist < hi), vals, NEG).reshape(-1, win)


def _cmp_tables(w1, pe):
    half = CMP_BLOCK // 2
    eye = jnp.eye(NSA_GROUPS, dtype=F32)
    w = w1.reshape(2, half, HEAD_DIM, CMP_HIDDEN)
    wexp = jnp.einsum('aldh,gk->algdkh', w, eye)
    wexp = wexp.reshape(2, half * NSA_GROUPS * HEAD_DIM, NSA_GROUPS * CMP_HIDDEN)
    pexp = jnp.broadcast_to(pe.reshape(2, half, 1, HEAD_DIM), (2, half, NSA_GROUPS, HEAD_DIM))
    return wexp[0].astype(BF16), wexp[1].astype(BF16), pexp.reshape(2, -1)


def _block_diag2(w):
    z = jnp.zeros_like(w)
    return jnp.concatenate([jnp.concatenate([w, z], axis=1), jnp.concatenate([z, w], axis=1)], axis=0)


def kernel(x, mem, norm_ffn1, w1_gate, w1_up, w1_down, norm_mix, w_in, cmp_pe_k, cmp_w1_k, cmp_w2_k,
           cmp_pe_v, cmp_w1_v, cmp_w2_v, attn_sinks, rel_bias, w_up_a, w_up_b, w_out, norm_xattn,
           norm_mem, w_xq, w_xkv, w_xo, norm_ffn2, w2_gate, w2_up, w2_down, norm_final):
    b, s, d = x.shape
    n = b * s
    assert norm_ffn1.shape[0] == 1, "single-layer kernel"
    assert s % FAR_CHUNK == 0 and s >= 2 * NSA_WINDOW
    n_cmp = (s - CMP_BLOCK) // CMP_STRIDE + 1
    ncp = s // CMP_STRIDE
    n_sel = s // SEL_BLOCK
    assert n_sel <= LANES, "selection blocks must fit the 128 mask lanes"
    bf = lambda w: w.astype(BF16)
    scale = HEAD_DIM ** -0.5
    log2e = math.log2(math.e)

    wi = w_in[0]
    nq = NSA_HEADS * HEAD_DIM
    nkv = NSA_GROUPS * HEAD_DIM
    o_g = nq + 6 * nkv
    o_qb = o_g + 3 * NSA_HEADS
    o_kb = o_qb + SWA_HEADS * HEAD_DIM
    o_ga = o_kb + 2 * SWA_KV_HEADS * HEAD_DIM
    wqa = bf(wi[:, :nq] * (scale * log2e))
    wqb = bf(wi[:, o_qb:o_kb] * (scale * log2e))
    wkv = bf(jnp.concatenate([wi[:, nq + 2 * nkv:o_g], wi[:, o_kb:o_ga]], axis=1))
    wc = bf(wi[:, nq:nq + 2 * nkv])
    wgn = bf(jnp.pad(wi[:, o_g:o_qb], ((0, 0), (0, LANES - 3 * NSA_HEADS))))
    wga = bf(wi[:, o_ga:o_ga + d])
    wgb = bf(wi[:, o_ga + d:o_ga + 2 * d])
    wka, wkb, pek = _cmp_tables(cmp_w1_k[0], cmp_pe_k[0])
    wva, wvb, pev = _cmp_tables(cmp_w1_v[0], cmp_pe_v[0])
    w2k = bf(_block_diag2(cmp_w2_k[0]))
    w2v = bf(_block_diag2(cmp_w2_v[0]))

    bias_a = rel_bias[:, :NSA_HEADS] * log2e
    bias_b = rel_bias[:, NSA_HEADS:] * log2e
    far_bias = _bias_of_dist(bias_a, jnp.full((1,), REL_MAX_DIST, jnp.int32))[:, 0]
    no_shift = jnp.zeros((SWA_HEADS,), F32)
    tw = _bias_tile(bias_a, far_bias, NSA_WIN_CHUNKS, NSA_WINDOW)
    ts = _bias_tile(bias_a, far_bias, NSA_WIN_CHUNKS, NSA_WIN_CHUNKS * TQ)
    tb = _bias_tile(bias_b, no_shift, SWA_WIN_CHUNKS, SWA_WINDOW)
    dist_c = (jnp.arange(TQ)[:, None] + (CMP_WIN_BACK * CMP_STRIDE - (CMP_BLOCK - 1))
              - CMP_STRIDE * jnp.arange(CMP_BLOCK)[None, :])
    dcv = jnp.where(dist_c < REL_MAX_DIST, _bias_of_dist(bias_a, dist_c) - far_bias[:, None, None], 0.0)
    dcv = jnp.where(dist_c >= 0, dcv, NEG).reshape(NSA_HEADS * TQ, CMP_BLOCK)
    dc_hi = dcv.astype(BF16)
    dc_lo = (dcv - dc_hi.astype(F32)).astype(BF16)
    after = jnp.full((NSA_HEADS * TQ, 1), NEG, BF16)
    ca = jnp.concatenate([dc_hi, dc_lo, after,
                          jnp.zeros((NSA_HEADS * TQ, LANES - 2 * CMP_BLOCK - 1), BF16)], axis=1)
    sink = jnp.broadcast_to(jnp.repeat(attn_sinks[0] * log2e, TQ)[:, None], (SWA_HEADS * TQ, LANES))

    cs = np.arange(ncp)[:, None] * CMP_STRIDE
    ss = np.arange(LANES)[None, :] * SEL_BLOCK
    ovl = np.maximum(np.minimum(cs + CMP_BLOCK, ss + SEL_BLOCK) - np.maximum(cs, ss), 0) / CMP_BLOCK
    ovl[n_cmp:] = 0.0
    ov = jnp.asarray(ovl, dtype=BF16)

    row = lambda g: g.reshape(1, -1)
    x1 = _ffn(x.reshape(n, d), row(norm_ffn1[0]), bf(w1_gate[0]), bf(w1_up[0]), bf(w1_down[0]),
              row(norm_final), final_norm=False)
    qa, qb, kv, kcf, vcf, gn = _inproj(x1, row(norm_mix[0]), wqa, wqb, wkv, wc, wgn, seq_len=s)
    kc, vc = _compress(kcf.reshape(b, s, LANES), vcf.reshape(b, s, LANES), pek, pev,
                       wka, wkb, wva, wvb, w2k, w2v, ov)
    oa, ob = _mix(qa.reshape(b, s, -1), qb.reshape(b, s, -1), gn.reshape(b, s, LANES), kc, vc,
                  kv.reshape(b, s, -1), ca, tw[:, :TQ], tw[:, -2 * TQ:], ts[:, -2 * TQ:], tb, sink,
                  n_sel=n_sel)
    x2 = _merge(x1, row(norm_mix[0]), oa.reshape(n, -1), ob.reshape(n, -1), wga, wgb,
                bf(w_up_a[0]), bf(w_up_b[0]), bf(w_out[0]))
    mkv = _memkv(mem.reshape(-1, d), row(norm_mem[0]), bf(w_xkv[0]))
    x3 = _xattn(x2.reshape(b, s, d), row(norm_xattn[0]), mkv.reshape(b, -1, 2 * d),
                bf(w_xq[0] * ((d // XATTN_HEADS) ** -0.5 * log2e)), bf(w_xo[0]))
    out = _ffn(x3.reshape(n, d), row(norm_ffn2[0]), bf(w2_gate[0]), bf(w2_up[0]), bf(w2_down[0]),
               row(norm_final), final_norm=True)
    return out.reshape(b, s, d)
```

```python
import functools
import math

import numpy as np
import jax
import jax.numpy as jnp
from jax import lax
from jax.experimental import pallas as pl
from jax.experimental.pallas import tpu as pltpu

F32 = jnp.float32
BF16 = jnp.bfloat16

HEAD_DIM = 64
NSA_HEADS = 8
NSA_GROUPS = 2
HEADS_PER_GROUP = NSA_HEADS // NSA_GROUPS
CMP_BLOCK = 32
CMP_STRIDE = 16
CMP_HIDDEN = 256
SEL_BLOCK = 64
SEL_TOPN = 16
NSA_WINDOW = 512
SWA_HEADS = 8
SWA_KV_HEADS = 2
SWA_WINDOW = 128
REL_BUCKETS = 32
REL_MAX_DIST = 128
XATTN_HEADS = 4
EPS = 1e-6
NEG = -1e30

LANES = 128
MXU_WIDTH = 256
FFN_ROWS = 1024
ROW_TILE = 512
XATTN_ROWS = 1024
TQ = 128
MIX_TILES = 2
FAR_CHUNK = 512
NSA_WIN_CHUNKS = NSA_WINDOW // TQ + 1
SWA_WIN_CHUNKS = SWA_WINDOW // TQ + 1
CMP_WIN_BACK = 16
VMEM_LIMIT = 56 * 1024 * 1024


def _rms(x, g):
    return x * lax.rsqrt(jnp.mean(x * x, axis=-1, keepdims=True) + EPS) * g


def _dot(a, b):
    return jnp.dot(a, b, preferred_element_type=F32)


def _dot_nt(a, b):
    return lax.dot_general(a, b, (((1,), (1,)), ((), ())), preferred_element_type=F32)


def _params(*sem):
    return pltpu.CompilerParams(dimension_semantics=sem, vmem_limit_bytes=VMEM_LIMIT)


def _ffn_kernel(x_ref, g_ref, wg_ref, wu_ref, wd_ref, gf_ref, o_ref, h_scr, acc_scr, *, final_norm):
    def down(h, j):
        a = _dot(h, wg_ref[j])
        u = _dot(h, wu_ref[j])
        return _dot((a * jax.nn.sigmoid(a) * u).astype(BF16), wd_ref[j])

    h0 = _rms(x_ref[...], g_ref[...]).astype(BF16)
    h_scr[...] = h0
    acc_scr[...] = down(h0, 0)

    def chunk(j, carry):
        acc_scr[...] += down(h_scr[...], j)
        return carry

    last = wg_ref.shape[0] - 1
    lax.fori_loop(1, last, chunk, 0)
    y = x_ref[...] + 0.5 * (acc_scr[...] + down(h_scr[...], last))
    if final_norm:
        y = _rms(y, gf_ref[...])
    o_ref[...] = y


def _ffn(x2d, g, wg, wu, wd, gf, *, final_norm):
    n, d = x2d.shape
    dff = wg.shape[1]
    tm = min(FFN_ROWS, n)
    tf = MXU_WIDTH
    nc = dff // tf
    assert nc >= 2, "first and last chunk are peeled"
    once = pl.Buffered(1)
    chunks = lambda shape: pl.BlockSpec(shape, lambda i: (0, 0, 0), pipeline_mode=once)
    return pl.pallas_call(
        functools.partial(_ffn_kernel, final_norm=final_norm),
        grid=(n // tm,),
        in_specs=[
            pl.BlockSpec((tm, d), lambda i: (i, 0)),
            pl.BlockSpec((1, d), lambda i: (0, 0)),
            chunks((nc, d, tf)), chunks((nc, d, tf)), chunks((nc, tf, d)),
            pl.BlockSpec((1, d), lambda i: (0, 0)),
        ],
        out_specs=pl.BlockSpec((tm, d), lambda i: (i, 0)),
        out_shape=jax.ShapeDtypeStruct((n, d), F32),
        scratch_shapes=[pltpu.VMEM((tm, d), BF16), pltpu.VMEM((tm, d), F32)],
        compiler_params=_params("parallel"),
        name="ffn",
    )(x2d, g, wg.reshape(d, nc, tf).transpose(1, 0, 2), wu.reshape(d, nc, tf).transpose(1, 0, 2),
      wd.reshape(nc, tf, d), gf)


def _expand_head_pairs(r, heads, kv_heads):
    lane = lax.broadcasted_iota(jnp.int32, (r.shape[0], LANES), 1)
    low = lane < HEAD_DIM
    tiles = []
    for p in range(heads // 2):
        pair = r[:, p * LANES:(p + 1) * LANES]
        swapped = pltpu.roll(pair, HEAD_DIM, 1)
        if (2 * p) // (heads // kv_heads) == 0:
            tiles += [jnp.where(low, pair, 0.0), jnp.where(low, swapped, 0.0)]
        else:
            tiles += [jnp.where(low, 0.0, swapped), jnp.where(low, 0.0, pair)]
    return tiles


def _compact_head_pairs(tiles, heads, kv_heads):
    lane = lax.broadcasted_iota(jnp.int32, tiles[0].shape, 1)
    low = lane < HEAD_DIM
    out = []
    for p in range(heads // 2):
        a, b = tiles[2 * p], tiles[2 * p + 1]
        if (2 * p) // (heads // kv_heads) == 0:
            out.append(jnp.where(low, a, pltpu.roll(b, HEAD_DIM, 1)))
        else:
            out.append(jnp.where(low, pltpu.roll(a, HEAD_DIM, 1), b))
    return jnp.concatenate(out, axis=1)


def _inproj_kernel(x_ref, g_ref, wqa_ref, wqb_ref, wkv_ref, wc_ref, wg_ref,
                   qa_ref, qb_ref, kv_ref, kc_ref, vc_ref, gn_ref, *, seq_len):
    tm = x_ref.shape[0]
    h = _rms(x_ref[...], g_ref[...]).astype(BF16)
    for w_ref, q_ref, heads, kv_heads in ((wqa_ref, qa_ref, NSA_HEADS, NSA_GROUPS),
                                          (wqb_ref, qb_ref, SWA_HEADS, SWA_KV_HEADS)):
        tiles = _expand_head_pairs(_dot(h, w_ref[...]), heads, kv_heads)
        for hh, t in enumerate(tiles):
            q_ref[:, hh * LANES:(hh + 1) * LANES] = t.astype(BF16)
    kv = _dot(h, wkv_ref[...]).astype(BF16)
    pos = (pl.program_id(0) * tm + lax.broadcasted_iota(jnp.int32, (tm, LANES), 0)) % seq_len
    lane = lax.broadcasted_iota(jnp.int32, (tm, LANES), 1)
    block_onehot = jnp.where(lane == pos // SEL_BLOCK, 1.0, 0.0).astype(BF16)
    ones = jnp.ones((tm, LANES), BF16)
    for i in range(kv.shape[1] // LANES):
        kv_ref[:, 2 * i * LANES:(2 * i + 1) * LANES] = kv[:, i * LANES:(i + 1) * LANES]
        kv_ref[:, (2 * i + 1) * LANES:(2 * i + 2) * LANES] = block_onehot if i % 2 == 0 else ones
    c = _dot(h, wc_ref[...])
    kc_ref[...] = c[:, :LANES]
    vc_ref[...] = c[:, LANES:]
    gn_ref[...] = jax.nn.sigmoid(_dot(h, wg_ref[...]))


def _inproj(x2d, g, wqa, wqb, wkv, wc, wg, *, seq_len):
    n, d = x2d.shape
    kv_w = 2 * wkv.shape[1]
    tm = min(ROW_TILE, n)
    full = lambda w: pl.BlockSpec(w.shape, lambda i: (0, 0))
    row = lambda width: pl.BlockSpec((tm, width), lambda i: (i, 0))
    return pl.pallas_call(
        functools.partial(_inproj_kernel, seq_len=seq_len),
        grid=(n // tm,),
        in_specs=[row(d), full(g), full(wqa), full(wqb), full(wkv), full(wc), full(wg)],
        out_specs=[row(2 * wqa.shape[1]), row(2 * wqb.shape[1]), row(kv_w), row(LANES), row(LANES),
                   row(LANES)],
        out_shape=[
            jax.ShapeDtypeStruct((n, 2 * wqa.shape[1]), BF16),
            jax.ShapeDtypeStruct((n, 2 * wqb.shape[1]), BF16),
            jax.ShapeDtypeStruct((n, kv_w), BF16),
            jax.ShapeDtypeStruct((n, LANES), F32),
            jax.ShapeDtypeStruct((n, LANES), F32),
            jax.ShapeDtypeStruct((n, LANES), F32),
        ],
        compiler_params=_params("parallel"),
        name="inproj",
    )(x2d, g, wqa, wqb, wkv, wc, wg)


def _compress_kernel(ak_ref, av_ref, pek_ref, pev_ref, wka_ref, wkb_ref, wva_ref, wvb_ref,
                     w2k_ref, w2v_ref, ov_ref, kc_ref, vc_ref):
    rows = kc_ref.shape[1]

    def one(a_ref, pe_ref, wa_ref, wb_ref, w2_ref):
        u = v = None
        for l in range(CMP_STRIDE):
            tok = a_ref[0, pl.ds(l, rows, stride=CMP_STRIDE), :]
            lanes = slice(l * LANES, (l + 1) * LANES)
            du = _dot((tok + pe_ref[0:1, lanes]).astype(BF16), wa_ref[lanes, :])
            dv = _dot((tok + pe_ref[1:2, lanes]).astype(BF16), wb_ref[lanes, :])
            u = du if u is None else u + du
            v = dv if v is None else v + dv
        hid = u + pltpu.roll(v, rows - 1, 0)
        hid = jax.nn.gelu(hid, approximate=True)
        return _dot(hid.astype(BF16), w2_ref[...]).astype(BF16)

    kc_ref[0] = one(ak_ref, pek_ref, wka_ref, wkb_ref, w2k_ref)
    vc_ref[0, :, :LANES] = one(av_ref, pev_ref, wva_ref, wvb_ref, w2v_ref)
    vc_ref[0, :, LANES:2 * LANES] = ov_ref[...]
    vc_ref[0, :, 2 * LANES:] = jnp.ones(ov_ref.shape, BF16)


def _compress(ak, av, pek, pev, wka, wkb, wva, wvb, w2k, w2v, ov):
    b, s, width = ak.shape
    ncp = s // CMP_STRIDE
    full = lambda w: pl.BlockSpec(w.shape, lambda i: (0,) * w.ndim)
    per_b = lambda w: pl.BlockSpec((1, ncp, w), lambda i: (i, 0, 0))
    tokens = pl.BlockSpec((1, s, width), lambda i: (i, 0, 0))
    return pl.pallas_call(
        _compress_kernel,
        grid=(b,),
        in_specs=[tokens, tokens, full(pek), full(pev), full(wka), full(wkb),
                  full(wva), full(wvb), full(w2k), full(w2v), full(ov)],
        out_specs=[per_b(LANES), per_b(3 * LANES)],
        out_shape=[jax.ShapeDtypeStruct((b, ncp, LANES), BF16),
                   jax.ShapeDtypeStruct((b, ncp, 3 * LANES), BF16)],
        compiler_params=_params("parallel"),
        name="compress",
    )(ak, av, pek, pev, wka, wkb, wva, wvb, w2k, w2v, ov)


def _stack_heads(q, heads):
    return jnp.concatenate([q[:, h * LANES:(h + 1) * LANES] for h in range(heads)], axis=0)


def _window_rows(ref, s0, window_chunks, fallback):
    parts = []
    for r in range(window_chunks):
        start = s0 + (r - (window_chunks - 1)) * TQ
        start = jnp.where(start < 0, fallback, start)
        parts.append(ref[0, pl.ds(pl.multiple_of(start, TQ), TQ), :])
    return jnp.concatenate(parts, axis=0)


def _lanes(a, b):
    return jnp.concatenate([a, b], axis=1)


def _per_head(tiles_per_group):
    return jnp.concatenate([tiles_per_group[h // HEADS_PER_GROUP] for h in range(NSA_HEADS)], axis=0)


def _mix_kernel(*refs, n_sel):
    def one_tile(u, carry):
        _mix_tile(u, *refs, n_sel=n_sel)
        return carry

    lax.fori_loop(0, MIX_TILES, one_tile, 0)


def _mix_tile(u, qa_ref, qb_ref, gn_ref, kc_ref, vc_ref, ks_ref, vs_ref, kw_ref, vw_ref, kb_ref, vb_ref,
                ca_ref, tw0_ref, tw1_ref, ts_ref, tb_ref, sink_ref,
                oa_ref, ob_ref, m_scr, acc_scr, sfa_scr, sfb_scr, *, n_sel):
    c = pl.program_id(1) * MIX_TILES + u
    tile_rows = pl.ds(pl.multiple_of(u * TQ, TQ), TQ)
    s0 = c * TQ
    rows = NSA_HEADS * TQ
    ncp = kc_ref.shape[1]
    nsp = LANES
    win = NSA_WIN_CHUNKS * TQ

    q = _stack_heads(qa_ref[0, tile_rows, :], NSA_HEADS)
    t_col = s0 + lax.broadcasted_iota(jnp.int32, (rows, LANES), 0) % TQ

    n_i = lax.broadcasted_iota(jnp.int32, (ncp, LANES), 0)
    lane_c = lax.broadcasted_iota(jnp.int32, (ncp, LANES), 1)
    w0 = s0 // CMP_STRIDE - CMP_WIN_BACK
    slot = jnp.where(lane_c < CMP_BLOCK, lane_c, lane_c - CMP_BLOCK)
    place = ((lane_c < 2 * CMP_BLOCK) & (n_i == w0 + slot)) | (
        (lane_c == 2 * CMP_BLOCK) & (n_i >= w0 + CMP_BLOCK))
    kc_aug = _lanes(kc_ref[0], jnp.where(place, 1.0, 0.0).astype(BF16))
    s_c = _dot_nt(_lanes(q, ca_ref[...]), kc_aug)
    e_c = jnp.exp2(s_c - jnp.max(s_c, axis=-1, keepdims=True)).astype(BF16)
    r_c = _dot(e_c, vc_ref[0])
    inv_c = jnp.where(t_col >= CMP_BLOCK - 1, 1.0 / r_c[:, 2 * LANES:], 0.0)
    o_c = r_c[:, :LANES] * inv_c
    imp_h = r_c[:, LANES:2 * LANES] * inv_c

    t_q = s0 + lax.broadcasted_iota(jnp.int32, (TQ, 1), 0)
    cur_q = t_q // SEL_BLOCK
    blk_l = lax.broadcasted_iota(jnp.int32, (TQ, nsp), 1)
    forced = (blk_l == 0) | (blk_l == cur_q) | (blk_l == cur_q - 1)
    causal_blk = blk_l * SEL_BLOCK <= t_q
    cand = causal_blk & jnp.logical_not(forced)
    n_sweeps = min(SEL_TOPN, n_sel) - 3
    blk_s = lax.broadcasted_iota(jnp.int32, (nsp, TQ), 0).astype(F32)
    first_blk = (s0 - NSA_WINDOW) // SEL_BLOCK

    near_q, far_q = [], []
    for g in range(NSA_GROUPS):
        base = g * HEADS_PER_GROUP * TQ
        imp = imp_h[base:base + TQ]
        for h in range(1, HEADS_PER_GROUP):
            imp = imp + imp_h[base + h * TQ:base + (h + 1) * TQ]
        v_t = jnp.where(cand, imp, -1.0).T
        for _ in range(n_sweeps):
            mx = jnp.max(v_t, axis=0, keepdims=True)
            first = jnp.min(jnp.where(v_t == mx, blk_s, float(nsp)), axis=0, keepdims=True)
            v_t = jnp.where(blk_s == first, -2.0, v_t)
        chosen = (forced | (v_t.T == -2.0)) & causal_blk
        near_q.append(jnp.where(chosen, 0.0, NEG).astype(BF16))
        far_q.append(jnp.where(chosen & (blk_l < first_blk), 0.0, NEG).astype(BF16))

    s_len = ks_ref.shape[1]
    spare = s_len - TQ
    causal_q = _per_head([jnp.where(causal_blk, 0.0, NEG).astype(BF16)] * NSA_GROUPS)

    def add_tiles(s, first_ref, last_ref):
        first = s[:, :TQ] if first_ref is None else s[:, :TQ] + first_ref[...]
        return jnp.concatenate([first, s[:, TQ:win - 2 * TQ], s[:, win - 2 * TQ:] + last_ref[...]], axis=1)

    s_w = _dot_nt(_lanes(q, causal_q), _window_rows(kw_ref, s0, NSA_WIN_CHUNKS, spare))
    s_w = add_tiles(s_w, tw0_ref, tw1_ref)
    e_w = jnp.exp2((s_w - jnp.max(s_w, axis=-1, keepdims=True)).astype(BF16))
    r_w = _dot(e_w, _window_rows(vw_ref, s0, NSA_WIN_CHUNKS, spare))
    o_w = r_w[:, :LANES] / r_w[:, LANES:]

    s_n = _dot_nt(_lanes(q, _per_head(near_q)), _window_rows(ks_ref, s0, NSA_WIN_CHUNKS, spare))
    s_n = add_tiles(s_n, None, ts_ref)
    m_n = jnp.broadcast_to(jnp.max(s_n, axis=-1, keepdims=True), m_scr.shape)
    m_scr[...] = m_n
    acc_scr[...] = _dot(jnp.exp2((s_n - jnp.concatenate([m_n] * NSA_WIN_CHUNKS, axis=1)).astype(BF16)),
                        _window_rows(vs_ref, s0, NSA_WIN_CHUNKS, spare))

    q_far = _lanes(q, _per_head(far_q))
    n_far = jnp.maximum(s0 - NSA_WINDOW + FAR_CHUNK - 1, 0) // FAR_CHUNK
    last_chunk = s_len // FAR_CHUNK - 1
    group_rows = [slice(g * rows // NSA_GROUPS, (g + 1) * rows // NSA_GROUPS) for g in range(NSA_GROUPS)]

    def far_rows(j):
        return pl.ds(pl.multiple_of(jnp.minimum(j, last_chunk) * FAR_CHUNK, FAR_CHUNK), FAR_CHUNK)

    def far_logits(j, buf):
        for r in group_rows:
            buf[r, :] = _dot_nt(q_far[r], ks_ref[0, far_rows(j), :])

    def far_update(buf, j):
        for r in group_rows:
            s_f = buf[r, :]
            m_old = m_scr[r, :]
            m_new = jnp.maximum(m_old, jnp.broadcast_to(jnp.max(s_f, axis=-1, keepdims=True), m_old.shape))
            e_f = jnp.exp2((s_f - jnp.concatenate([m_new] * (FAR_CHUNK // LANES), axis=1)).astype(BF16))
            alpha = jnp.exp2(m_old - m_new)
            acc_scr[r, :] = _lanes(alpha, alpha) * acc_scr[r, :] + _dot(e_f, vs_ref[0, far_rows(j), :])
            m_scr[r, :] = m_new

    far_logits(0, sfa_scr)

    def far_step(i, carry):
        far_logits(2 * i + 1, sfb_scr)
        far_update(sfa_scr, 2 * i)
        far_logits(2 * i + 2, sfa_scr)
        far_update(sfb_scr, 2 * i + 1)
        return carry

    lax.fori_loop(0, n_far // 2, far_step, 0)

    @pl.when(n_far % 2 == 1)
    def _():
        far_update(sfa_scr, n_far - 1)

    o_s = acc_scr[:, :LANES] / acc_scr[:, LANES:]

    gn = gn_ref[0, tile_rows, :]
    tiles = []
    for h in range(NSA_HEADS):
        r = slice(h * TQ, (h + 1) * TQ)
        tiles.append(gn[:, h:h + 1] * o_c[r] + gn[:, NSA_HEADS + h:NSA_HEADS + h + 1] * o_s[r]
                     + gn[:, 2 * NSA_HEADS + h:2 * NSA_HEADS + h + 1] * o_w[r])
    oa_ref[0, tile_rows, :] = _compact_head_pairs(tiles, NSA_HEADS, NSA_GROUPS).astype(BF16)

    s_b = _dot_nt(_lanes(_stack_heads(qb_ref[0, tile_rows, :], SWA_HEADS), causal_q),
                  _window_rows(kb_ref, s0, SWA_WIN_CHUNKS, spare)) + tb_ref[...]
    sink = sink_ref[...]
    m_b = jnp.maximum(jnp.broadcast_to(jnp.max(s_b, axis=-1, keepdims=True), sink.shape), sink)
    e_b = jnp.exp2((s_b - jnp.concatenate([m_b] * SWA_WIN_CHUNKS, axis=1)).astype(BF16))
    r_b = _dot(e_b, _window_rows(vb_ref, s0, SWA_WIN_CHUNKS, spare))
    o_b = r_b[:, :LANES] / (r_b[:, LANES:] + jnp.exp2(sink - m_b))
    ob_ref[0, tile_rows, :] = _compact_head_pairs([o_b[h * TQ:(h + 1) * TQ] for h in range(SWA_HEADS)],
                                                  SWA_HEADS, SWA_KV_HEADS).astype(BF16)


def _mix(qa, qb, gn, kc, vc, kv, ca, tw0, tw1, ts, tb, sink, *, n_sel):
    b, s, qw = qa.shape
    ncp = kc.shape[1]
    rows = NSA_HEADS * TQ
    once = pl.Buffered(1)
    full = lambda w: pl.BlockSpec(w.shape, lambda i, j: (0,) * w.ndim, pipeline_mode=once)
    per_b = lambda blk, col, w: pl.BlockSpec((1, blk, w), lambda i, j: (i, 0, col), pipeline_mode=once)
    tile = lambda w: pl.BlockSpec((1, MIX_TILES * TQ, w), lambda i, j: (i, j, 0))
    return pl.pallas_call(
        functools.partial(_mix_kernel, n_sel=n_sel),
        grid=(b, s // (MIX_TILES * TQ)),
        in_specs=[tile(qw), tile(qw), tile(LANES), per_b(ncp, 0, LANES), per_b(ncp, 0, 3 * LANES)]
        + [per_b(s, col, 2 * LANES) for col in range(6)]
        + [full(ca), full(tw0), full(tw1), full(ts), full(tb), full(sink)],
        out_specs=[tile(qw // 2), tile(qw // 2)],
        out_shape=[jax.ShapeDtypeStruct((b, s, qw // 2), BF16)] * 2,
        scratch_shapes=[pltpu.VMEM((rows, LANES), F32), pltpu.VMEM((rows, 2 * LANES), F32),
                        pltpu.VMEM((rows, FAR_CHUNK), F32), pltpu.VMEM((rows, FAR_CHUNK), F32)],
        compiler_params=_params("parallel", "arbitrary"),
        name="mix",
    )(qa, qb, gn, kc, vc, kv, kv, kv, kv, kv, kv, ca, tw0, tw1, ts, tb, sink)


def _merge_kernel(x_ref, g_ref, oa_ref, ob_ref, wga_ref, wgb_ref, wua_ref, wub_ref, wo_ref, o_ref):
    x = x_ref[...]
    h = _rms(x, g_ref[...]).astype(BF16)
    merged = (jax.nn.sigmoid(_dot(h, wga_ref[...])) * _dot(oa_ref[...], wua_ref[...])
              + jax.nn.sigmoid(_dot(h, wgb_ref[...])) * _dot(ob_ref[...], wub_ref[...]))
    o_ref[...] = x + _dot(merged.astype(BF16), wo_ref[...])


def _merge(x2d, g, oa, ob, wga, wgb, wua, wub, wo):
    n, d = x2d.shape
    tm = min(ROW_TILE, n)
    full = lambda w: pl.BlockSpec(w.shape, lambda i: (0, 0))
    row = lambda width: pl.BlockSpec((tm, width), lambda i: (i, 0))
    return pl.pallas_call(
        _merge_kernel,
        grid=(n // tm,),
        in_specs=[row(d), full(g), row(oa.shape[1]), row(ob.shape[1]), full(wga), full(wgb),
                  full(wua), full(wub), full(wo)],
        out_specs=row(d),
        out_shape=jax.ShapeDtypeStruct((n, d), F32),
        compiler_params=_params("parallel"),
        name="merge",
    )(x2d, g, oa, ob, wga, wgb, wua, wub, wo)


def _memkv_kernel(m_ref, g_ref, w_ref, o_ref):
    o_ref[...] = _dot(_rms(m_ref[...], g_ref[...]).astype(BF16), w_ref[...]).astype(BF16)


def _memkv(m2d, g, w):
    n, d = m2d.shape
    tm = min(ROW_TILE, n)
    return pl.pallas_call(
        _memkv_kernel,
        grid=(n // tm,),
        in_specs=[pl.BlockSpec((tm, d), lambda i: (i, 0)), pl.BlockSpec(g.shape, lambda i: (0, 0)),
                  pl.BlockSpec(w.shape, lambda i: (0, 0))],
        out_specs=pl.BlockSpec((tm, w.shape[1]), lambda i: (i, 0)),
        out_shape=jax.ShapeDtypeStruct((n, w.shape[1]), BF16),
        compiler_params=_params("parallel"),
        name="memkv",
    )(m2d, g, w)


def _xattn_kernel(x_ref, g_ref, kv_ref, wq_ref, wo_ref, o_ref):
    x = x_ref[0]
    d = x.shape[1]
    hd = d // XATTN_HEADS
    q = _dot(_rms(x, g_ref[...]).astype(BF16), wq_ref[...]).astype(BF16)
    outs = []
    for h in range(XATTN_HEADS):
        k = kv_ref[0, :, h * hd:(h + 1) * hd]
        v = kv_ref[0, :, d + h * hd:d + (h + 1) * hd]
        s = _dot_nt(q[:, h * hd:(h + 1) * hd], k)
        e = jnp.exp2(s - jnp.max(s, axis=-1, keepdims=True))
        outs.append((_dot(e.astype(BF16), v) / jnp.sum(e, axis=-1, keepdims=True)).astype(BF16))
    o_ref[0] = x + _dot(jnp.concatenate(outs, axis=1), wo_ref[...])


def _xattn(x3d, g, kv, wq, wo):
    b, s, d = x3d.shape
    tm = min(XATTN_ROWS, s)
    full = lambda w: pl.BlockSpec(w.shape, lambda i, j: (0, 0))
    return pl.pallas_call(
        _xattn_kernel,
        grid=(b, s // tm),
        in_specs=[pl.BlockSpec((1, tm, d), lambda i, j: (i, j, 0)), full(g),
                  pl.BlockSpec((1,) + kv.shape[1:], lambda i, j: (i, 0, 0)), full(wq), full(wo)],
        out_specs=pl.BlockSpec((1, tm, d), lambda i, j: (i, j, 0)),
        out_shape=jax.ShapeDtypeStruct((b, s, d), F32),
        compiler_params=_params("parallel", "parallel"),
        name="xattn",
    )(x3d, g, kv, wq, wo)


def _rel_bucket(dist):
    dist = jnp.maximum(dist, 0)
    max_exact = REL_BUCKETS // 2
    d = jnp.maximum(dist, 1).astype(F32)
    large = max_exact + (jnp.log(d / max_exact) / math.log(REL_MAX_DIST / max_exact)
                         * (REL_BUCKETS - max_exact)).astype(jnp.int32)
    large = jnp.minimum(large, REL_BUCKETS - 1)
    return jnp.where(dist < max_exact, dist, large)


def _bias_of_dist(bias, dist):
    bucket = _rel_bucket(dist)[None]
    out = jnp.zeros((bias.shape[1],) + dist.shape, F32)
    for k in range(REL_BUCKETS):
        out = jnp.where(bucket == k, bias[k].reshape((-1,) + (1,) * dist.ndim), out)
    return out


def _bias_tile(bias, shift, window_chunks, hi):
    win = window_chunks * TQ
    dist = jnp.arange(TQ)[:, None] + (window_chunks - 1) * TQ - jnp.arange(win)[None, :]
    vals = _bias_of_dist(bias, dist) - shift[:, None, None]
    return jnp.where((dist >= 0) & (dist < hi), vals, NEG).reshape(-1, win)


def _cmp_tables(w1, pe):
    half = CMP_BLOCK // 2
    eye = jnp.eye(NSA_GROUPS, dtype=F32)
    w = w1.reshape(2, half, HEAD_DIM, CMP_HIDDEN)
    wexp = jnp.einsum('aldh,gk->algdkh', w, eye)
    wexp = wexp.reshape(2, half * NSA_GROUPS * HEAD_DIM, NSA_GROUPS * CMP_HIDDEN)
    pexp = jnp.broadcast_to(pe.reshape(2, half, 1, HEAD_DIM), (2, half, NSA_GROUPS, HEAD_DIM))
    return wexp[0].astype(BF16), wexp[1].astype(BF16), pexp.reshape(2, -1)


def _block_diag2(w):
    z = jnp.zeros_like(w)
    return jnp.concatenate([jnp.concatenate([w, z], axis=1), jnp.concatenate([z, w], axis=1)], axis=0)


def kernel(x, mem, norm_ffn1, w1_gate, w1_up, w1_down, norm_mix, w_in, cmp_pe_k, cmp_w1_k, cmp_w2_k,
           cmp_pe_v, cmp_w1_v, cmp_w2_v, attn_sinks, rel_bias, w_up_a, w_up_b, w_out, norm_xattn,
           norm_mem, w_xq, w_xkv, w_xo, norm_ffn2, w2_gate, w2_up, w2_down, norm_final):
    b, s, d = x.shape
    n = b * s
    assert norm_ffn1.shape[0] == 1, "single-layer kernel"
    assert s % FAR_CHUNK == 0 and s >= 2 * NSA_WINDOW
    n_cmp = (s - CMP_BLOCK) // CMP_STRIDE + 1
    ncp = s // CMP_STRIDE
    n_sel = s // SEL_BLOCK
    assert n_sel <= LANES, "selection blocks must fit the 128 mask lanes"
    bf = lambda w: w.astype(BF16)
    scale = HEAD_DIM ** -0.5
    log2e = math.log2(math.e)

    wi = w_in[0]
    nq = NSA_HEADS * HEAD_DIM
    nkv = NSA_GROUPS * HEAD_DIM
    o_g = nq + 6 * nkv
    o_qb = o_g + 3 * NSA_HEADS
    o_kb = o_qb + SWA_HEADS * HEAD_DIM
    o_ga = o_kb + 2 * SWA_KV_HEADS * HEAD_DIM
    wqa = bf(wi[:, :nq] * (scale * log2e))
    wqb = bf(wi[:, o_qb:o_kb] * (scale * log2e))
    wkv = bf(jnp.concatenate([wi[:, nq + 2 * nkv:o_g], wi[:, o_kb:o_ga]], axis=1))
    wc = bf(wi[:, nq:nq + 2 * nkv])
    wgn = bf(jnp.pad(wi[:, o_g:o_qb], ((0, 0), (0, LANES - 3 * NSA_HEADS))))
    wga = bf(wi[:, o_ga:o_ga + d])
    wgb = bf(wi[:, o_ga + d:o_ga + 2 * d])
    wka, wkb, pek = _cmp_tables(cmp_w1_k[0], cmp_pe_k[0])
    wva, wvb, pev = _cmp_tables(cmp_w1_v[0], cmp_pe_v[0])
    w2k = bf(_block_diag2(cmp_w2_k[0]))
    w2v = bf(_block_diag2(cmp_w2_v[0]))

    bias_a = rel_bias[:, :NSA_HEADS] * log2e
    bias_b = rel_bias[:, NSA_HEADS:] * log2e
    far_bias = _bias_of_dist(bias_a, jnp.full((1,), REL_MAX_DIST, jnp.int32))[:, 0]
    no_shift = jnp.zeros((SWA_HEADS,), F32)
    tw = _bias_tile(bias_a, far_bias, NSA_WIN_CHUNKS, NSA_WINDOW)
    ts = _bias_tile(bias_a, far_bias, NSA_WIN_CHUNKS, NSA_WIN_CHUNKS * TQ)
    tb = _bias_tile(bias_b, no_shift, SWA_WIN_CHUNKS, SWA_WINDOW)
    dist_c = (jnp.arange(TQ)[:, None] + (CMP_WIN_BACK * CMP_STRIDE - (CMP_BLOCK - 1))
              - CMP_STRIDE * jnp.arange(CMP_BLOCK)[None, :])
    dcv = jnp.where(dist_c < REL_MAX_DIST, _bias_of_dist(bias_a, dist_c) - far_bias[:, None, None], 0.0)
    dcv = jnp.where(dist_c >= 0, dcv, NEG).reshape(NSA_HEADS * TQ, CMP_BLOCK)
    dc_hi = dcv.astype(BF16)
    dc_lo = (dcv - dc_hi.astype(F32)).astype(BF16)
    after = jnp.full((NSA_HEADS * TQ, 1), NEG, BF16)
    ca = jnp.concatenate([dc_hi, dc_lo, after,
                          jnp.zeros((NSA_HEADS * TQ, LANES - 2 * CMP_BLOCK - 1), BF16)], axis=1)
    sink = jnp.broadcast_to(jnp.repeat(attn_sinks[0] * log2e, TQ)[:, None], (SWA_HEADS * TQ, LANES))

    cs = np.arange(ncp)[:, None] * CMP_STRIDE
    ss = np.arange(LANES)[None, :] * SEL_BLOCK
    ovl = np.maximum(np.minimum(cs + CMP_BLOCK, ss + SEL_BLOCK) - np.maximum(cs, ss), 0) / CMP_BLOCK
    ovl[n_cmp:] = 0.0
    ov = jnp.asarray(ovl, dtype=BF16)

    row = lambda g: g.reshape(1, -1)
    x1 = _ffn(x.reshape(n, d), row(norm_ffn1[0]), bf(w1_gate[0]), bf(w1_up[0]), bf(w1_down[0]),
              row(norm_final), final_norm=False)
    qa, qb, kv, kcf, vcf, gn = _inproj(x1, row(norm_mix[0]), wqa, wqb, wkv, wc, wgn, seq_len=s)
    kc, vc = _compress(kcf.reshape(b, s, LANES), vcf.reshape(b, s, LANES), pek, pev,
                       wka, wkb, wva, wvb, w2k, w2v, ov)
    oa, ob = _mix(qa.reshape(b, s, -1), qb.reshape(b, s, -1), gn.reshape(b, s, LANES), kc, vc,
                  kv.reshape(b, s, -1), ca, tw[:, :TQ], tw[:, -2 * TQ:], ts[:, -2 * TQ:], tb, sink,
                  n_sel=n_sel)
    x2 = _merge(x1, row(norm_mix[0]), oa.reshape(n, -1), ob.reshape(n, -1), wga, wgb,
                bf(w_up_a[0]), bf(w_up_b[0]), bf(w_out[0]))
    mkv = _memkv(mem.reshape(-1, d), row(norm_mem[0]), bf(w_xkv[0]))
    x3 = _xattn(x2.reshape(b, s, d), row(norm_xattn[0]), mkv.reshape(b, -1, 2 * d),
                bf(w_xq[0] * ((d // XATTN_HEADS) ** -0.5 * log2e)), bf(w_xo[0]))
    out = _ffn(x3.reshape(n, d), row(norm_ffn2[0]), bf(w2_gate[0]), bf(w2_up[0]), bf(w2_down[0]),
               row(norm_final), final_norm=True)
    return out.reshape(b, s, d)
```
